```python
import math
import jax, jax.numpy as jnp
from jax import lax
import numpy as np

D_MODEL = 1024
BATCH = 8
SEQ = 4096
DEPTH = 1

CHUNK = 64
Q_BLOCK = 128
D_MIX = D_MODEL
ATTN_HEADS = 8
HEAD_DIM = 64
D_ATTN = ATTN_HEADS * HEAD_DIM
D_SSM = D_MIX - D_ATTN
SSM_GROUP = 16
SSM_GROUPS = D_SSM // SSM_GROUP
SSM_STATE = 64
D_FF = -(-(8 * D_MODEL) // (3 * 256)) * 256
D_IN = 3 * D_ATTN + D_SSM
EPS = 1e-6
DT_MIN = 1e-3
DT_MAX = 1e-1

kernel_name = "hymba_stickbreaking_s5_block"


def rmsnorm(x, g):
    xf = x.astype(jnp.float32)
    xf = xf * lax.rsqrt(jnp.mean(xf * xf, axis=-1, keepdims=True) + EPS)
    return xf.astype(x.dtype) * g


def stick_breaking_attention(q, k, v):
    seq_len = q.shape[2]
    scale = HEAD_DIM ** -0.5
    outs = []
    for blk in range(seq_len // Q_BLOCK):
        q0 = blk * Q_BLOCK
        kv_len = q0 + Q_BLOCK
        qb = q[:, :, q0:kv_len]
        kb = k[:, :, :kv_len]
        vb = v[:, :, :kv_len]
        z = jnp.einsum('bhqd,bhkd->bhqk', qb, kb).astype(jnp.float32) * scale
        t_pos = q0 + jnp.arange(Q_BLOCK)[:, None]
        s_pos = jnp.arange(kv_len)[None, :]
        before = s_pos < t_pos
        log1m = jnp.where(before, -jax.nn.softplus(z), 0.0)
        rc = lax.cumsum(log1m, axis=3, reverse=True)
        log_w = jax.nn.log_sigmoid(z) + (rc - log1m)
        w = jnp.where(before, jnp.exp(log_w), 0.0)
        outs.append(jnp.einsum('bhqk,bhkd->bhqd', w.astype(vb.dtype), vb))
    return jnp.concatenate(outs, axis=2)


def _ssm_combine(a, b):
    a_lr, a_li, a_xr, a_xi = a
    b_lr, b_li, b_xr, b_xi = b
    lr = a_lr * b_lr - a_li * b_li
    li = a_lr * b_li + a_li * b_lr
    xr = b_lr * a_xr - b_li * a_xi + b_xr
    xi = b_lr * a_xi + b_li * a_xr + b_xi
    return (lr, li, xr, xi)


def s5_ssm(u, lambda_re, lambda_im, log_step, b_re, b_im, c_re, c_im, d_skip):
    bsz, seq_len, _ = u.shape
    ug = u.reshape(bsz, seq_len, SSM_GROUPS, SSM_GROUP).astype(jnp.float32)
    lam_re = lambda_re.astype(jnp.float32)
    lam_im = lambda_im.astype(jnp.float32)
    dt = jnp.exp(log_step.astype(jnp.float32))[:, None]
    mag = jnp.exp(lam_re * dt)
    ang = lam_im * dt
    lb_re = mag * jnp.cos(ang)
    lb_im = mag * jnp.sin(ang)
    den = lam_re * lam_re + lam_im * lam_im
    num_re = lb_re - 1.0
    f_re = (num_re * lam_re + lb_im * lam_im) / den
    f_im = (lb_im * lam_re - num_re * lam_im) / den
    br = b_re.astype(jnp.float32)
    bi = b_im.astype(jnp.float32)
    bb_re = f_re[..., None] * br - f_im[..., None] * bi
    bb_im = f_re[..., None] * bi + f_im[..., None] * br
    bu_re = jnp.einsum('blgh,gph->blgp', ug, bb_re)
    bu_im = jnp.einsum('blgh,gph->blgp', ug, bb_im)
    a_re = jnp.broadcast_to(lb_re, bu_re.shape)
    a_im = jnp.broadcast_to(lb_im, bu_im.shape)
    _, _, x_re, x_im = lax.associative_scan(_ssm_combine, (a_re, a_im, bu_re, bu_im), axis=1)
    y = (jnp.einsum('blgp,ghp->blgh', x_re, c_re.astype(jnp.float32))
         - jnp.einsum('blgp,ghp->blgh', x_im, c_im.astype(jnp.float32))
         + d_skip.astype(jnp.float32) * ug)
    return y.reshape(bsz, seq_len, D_SSM).astype(u.dtype)


def setup_inputs(seed: int = 0) -> dict:
    key = jax.random.key(seed)
    ks = jax.random.split(key, 24)
    f32 = jnp.float32
    G, P, H = SSM_GROUPS, SSM_STATE, SSM_GROUP
    x = jax.random.normal(ks[0], (BATCH, SEQ, D_MODEL), f32)
    norm1_g = 1.0 + 0.02 * jax.random.normal(ks[1], (DEPTH, D_MODEL), f32)
    w_in = jax.random.normal(ks[2], (DEPTH, D_MODEL, D_IN), f32) * D_MODEL ** -0.5
    attn_norm_g = 1.0 + 0.02 * jax.random.normal(ks[3], (DEPTH, D_ATTN), f32)
    lambda_re = -0.5 + 0.01 * jax.random.normal(ks[4], (DEPTH, G, P), f32)
    lambda_im = (math.pi * jnp.arange(P, dtype=f32))[None, None, :] + 0.01 * jax.random.normal(ks[5], (DEPTH, G, P), f32)
    log_step = jax.random.uniform(ks[6], (DEPTH, G), f32, math.log(DT_MIN), math.log(DT_MAX))
    b_re = jax.random.normal(ks[7], (DEPTH, G, P, H), f32) * (2.0 * H) ** -0.5
    b_im = jax.random.normal(ks[8], (DEPTH, G, P, H), f32) * (2.0 * H) ** -0.5
    c_re = jax.random.normal(ks[9], (DEPTH, G, H, P), f32) * (2.0 * P) ** -0.5
    c_im = jax.random.normal(ks[10], (DEPTH, G, H, P), f32) * (2.0 * P) ** -0.5
    d_skip = jax.random.normal(ks[11], (DEPTH, G, H), f32)
    w_glu = jax.random.normal(ks[12], (DEPTH, D_SSM, D_SSM), f32) * D_SSM ** -0.5
    ssm_norm_g = 1.0 + 0.02 * jax.random.normal(ks[13], (DEPTH, D_SSM), f32)
    w_out = jax.random.normal(ks[14], (DEPTH, D_MIX, D_MODEL), f32) * D_MIX ** -0.5
    norm2_g = 1.0 + 0.02 * jax.random.normal(ks[15], (DEPTH, D_MODEL), f32)
    w_gate = jax.random.normal(ks[16], (DEPTH, D_MODEL, D_FF), f32) * D_MODEL ** -0.5
    w_up = jax.random.normal(ks[17], (DEPTH, D_MODEL, D_FF), f32) * D_MODEL ** -0.5
    w_down = jax.random.normal(ks[18], (DEPTH, D_FF, D_MODEL), f32) * D_FF ** -0.5
    final_norm_g = 1.0 + 0.02 * jax.random.normal(ks[19], (D_MODEL,), f32)
    return {"x": x, "norm1_g": norm1_g, "w_in": w_in, "attn_norm_g": attn_norm_g,
            "lambda_re": lambda_re, "lambda_im": lambda_im, "log_step": log_step,
            "b_re": b_re, "b_im": b_im, "c_re": c_re, "c_im": c_im, "d_skip": d_skip,
            "w_glu": w_glu, "ssm_norm_g": ssm_norm_g, "w_out": w_out, "norm2_g": norm2_g,
            "w_gate": w_gate, "w_up": w_up, "w_down": w_down, "final_norm_g": final_norm_g}


def reference(x, norm1_g, w_in, attn_norm_g, lambda_re, lambda_im, log_step, b_re, b_im,
              c_re, c_im, d_skip, w_glu, ssm_norm_g, w_out, norm2_g, w_gate, w_up, w_down,
              final_norm_g):
    bsz, seq_len, _ = x.shape
    for i in range(DEPTH):
        h = rmsnorm(x, norm1_g[i])
        proj = h @ w_in[i]
        q = proj[..., :D_ATTN]
        k = proj[..., D_ATTN:2 * D_ATTN]
        v = proj[..., 2 * D_ATTN:3 * D_ATTN]
        u = proj[..., 3 * D_ATTN:]
        to_heads = lambda t: t.reshape(bsz, seq_len, ATTN_HEADS, HEAD_DIM).transpose(0, 2, 1, 3)
        o_attn = stick_breaking_attention(to_heads(q), to_heads(k), to_heads(v))
        o_attn = o_attn.transpose(0, 2, 1, 3).reshape(bsz, seq_len, D_ATTN)
        o_attn = rmsnorm(o_attn, attn_norm_g[i])
        y = s5_ssm(u, lambda_re[i], lambda_im[i], log_step[i], b_re[i], b_im[i],
                   c_re[i], c_im[i], d_skip[i])
        y = jax.nn.gelu(y)
        y = y * jax.nn.sigmoid(y @ w_glu[i])
        o_ssm = rmsnorm(y, ssm_norm_g[i])
        x = x + jnp.concatenate([o_attn, o_ssm], axis=-1) @ w_out[i]
        h = rmsnorm(x, norm2_g[i])
        x = x + (jax.nn.silu(h @ w_gate[i]) * (h @ w_up[i])) @ w_down[i]
    return rmsnorm(x, final_norm_g)
```

```python
import functools

import jax
import jax.numpy as jnp
from jax import lax
from jax.experimental import pallas as pl
from jax.experimental.pallas import tpu as pltpu

F32 = jnp.float32
BF16 = jnp.bfloat16

D_MODEL = 1024
ATTN_HEADS = 8
HEAD_DIM = 64
D_ATTN = ATTN_HEADS * HEAD_DIM
D_SSM = D_MODEL - D_ATTN
SSM_GROUP = 16
SSM_GROUPS = D_SSM // SSM_GROUP
SSM_STATE = 64
N_STATE = SSM_GROUPS * SSM_STATE
D_FF = 2816
D_IN = 3 * D_ATTN + D_SSM
EPS = 1e-6

LANES = 128
SUBLANES = 8
MXU_DIM = 256
VMEM_LIMIT_BYTES = 56 * 1024 * 1024

TM_PROJ = 512
TM_POST = 512
FF_CHUNK = 256
BLK = 128
HEAD_PAIRS = ATTN_HEADS // 2
SCAN_T = 64
SCAN_COLS = 512
K_SLICES = D_SSM // MXU_DIM
STATE_PER_SLICE = N_STATE // K_SLICES
LOG_W_ZERO = -104.0


def _rms(x):
    return x * lax.rsqrt(jnp.mean(x * x, axis=-1, keepdims=True) + EPS)


def _ssm_prep_kernel(lre_ref, lim_ref, ls_ref, bre_ref, bim_ref,
                     lbre_ref, lbim_ref, bbre_ref, bbim_ref):
    lam_re = lre_ref[...]
    lam_im = lim_ref[...]
    dt = jnp.exp(ls_ref[...])
    mag = jnp.exp(lam_re * dt)
    ang = lam_im * dt
    lb_re = mag * jnp.cos(ang)
    lb_im = mag * jnp.sin(ang)
    den = lam_re * lam_re + lam_im * lam_im
    num_re = lb_re - 1.0
    f_re = (num_re * lam_re + lb_im * lam_im) / den
    f_im = (lb_im * lam_re - num_re * lam_im) / den
    lbre_ref[...] = lb_re
    lbim_ref[...] = lb_im
    br = bre_ref[...]
    bi = bim_ref[...]
    bbre_ref[...] = f_re[:, None, :] * br - f_im[:, None, :] * bi
    bbim_ref[...] = f_re[:, None, :] * bi + f_im[:, None, :] * br


def _ssm_prep(lambda_re, lambda_im, log_step, b_re_t, b_im_t):
    g, p = lambda_re.shape
    h = b_re_t.shape[1]
    return pl.pallas_call(
        _ssm_prep_kernel,
        out_shape=(jax.ShapeDtypeStruct((g, p), F32), jax.ShapeDtypeStruct((g, p), F32),
                   jax.ShapeDtypeStruct((g, h, p), F32), jax.ShapeDtypeStruct((g, h, p), F32)),
        name="ssm_prep",
    )(lambda_re, lambda_im, log_step, b_re_t, b_im_t)


def _in_proj_kernel(x_ref, g_ref, w_ref, q_ref, k_ref, v_ref, u_ref):
    h = (_rms(x_ref[...]) * g_ref[...]).astype(BF16)
    scale = HEAD_DIM ** -0.5
    q = jnp.dot(h, w_ref[:, 0:D_ATTN], preferred_element_type=F32)
    q_ref[...] = (q * scale).astype(BF16)
    k = jnp.dot(h, w_ref[:, D_ATTN:2 * D_ATTN], preferred_element_type=F32)
    k_ref[...] = k.astype(BF16)
    v = jnp.dot(h, w_ref[:, 2 * D_ATTN:3 * D_ATTN], preferred_element_type=F32)
    v_ref[...] = v.astype(BF16)
    u_ref[...] = jnp.dot(h, w_ref[:, 3 * D_ATTN:], preferred_element_type=F32)


def _in_proj(x2, g, w):
    n = x2.shape[0]
    row = lambda i: (i, 0)
    fixed = lambda i: (0, 0)
    return pl.pallas_call(
        _in_proj_kernel,
        grid=(n // TM_PROJ,),
        in_specs=[pl.BlockSpec((TM_PROJ, D_MODEL), row),
                  pl.BlockSpec((1, D_MODEL), fixed),
                  pl.BlockSpec((D_MODEL, D_IN), fixed)],
        out_specs=[pl.BlockSpec((TM_PROJ, D_ATTN), row)] * 3 + [pl.BlockSpec((TM_PROJ, D_SSM), row)],
        out_shape=[jax.ShapeDtypeStruct((n, D_ATTN), BF16)] * 3 + [jax.ShapeDtypeStruct((n, D_SSM), F32)],
        compiler_params=pltpu.CompilerParams(dimension_semantics=("arbitrary",),
                                             vmem_limit_bytes=VMEM_LIMIT_BYTES),
        name="in_proj",
    )(x2, g, w)


def _attn_kernel(q_ref, k_ref, v_ref, g_ref, cmat_ref, o_ref, qm_scr, r_scr, acc_scr):
    i = pl.program_id(1)
    row = lax.broadcasted_iota(jnp.int32, (BLK, BLK), 0)
    col = lax.broadcasted_iota(jnp.int32, (BLK, BLK), 1)
    before = col < row
    low_half = col < HEAD_DIM
    cmat = cmat_ref[...]

    for p in range(HEAD_PAIRS):
        qp = q_ref[:, p * LANES:(p + 1) * LANES]
        zero = jnp.zeros_like(qp)
        qm_scr[2 * p] = jnp.where(low_half, qp, zero)
        qm_scr[2 * p + 1] = jnp.where(low_half, zero, qp)
        acc_scr[p] = jnp.zeros((BLK, LANES), F32)
    for h in range(ATTN_HEADS):
        r_scr[h] = jnp.zeros((BLK, BLK), F32)

    def block_step(j, diag):
        ks = pl.multiple_of(j * BLK, BLK)
        for p in range(HEAD_PAIRS):
            kp = k_ref[pl.ds(ks, BLK), p * LANES:(p + 1) * LANES]
            vp = v_ref[pl.ds(ks, BLK), p * LANES:(p + 1) * LANES]
            vzero = jnp.zeros_like(vp)
            for hh in range(2):
                h = 2 * p + hh
                z = lax.dot_general(qm_scr[h], kp, (((1,), (1,)), ((), ())),
                                    preferred_element_type=F32)
                sp = jnp.maximum(z, 0.0) + jnp.log(1.0 + jnp.exp(-jnp.abs(z)))
                log_beta = z - sp
                log1m = -sp
                if diag:
                    log1m = jnp.where(before, log1m, 0.0)
                hi = log1m.astype(BF16)
                lo = (log1m - hi.astype(F32)).astype(BF16)
                cs = jnp.dot(jnp.concatenate([hi, lo], axis=1), cmat,
                             preferred_element_type=F32)
                r = r_scr[h]
                w = jnp.exp(log_beta + cs[:, :BLK] + r)
                if diag:
                    w = jnp.where(before, w, 0.0)
                vh = jnp.where(low_half, vp, vzero) if hh == 0 else jnp.where(low_half, vzero, vp)
                acc_scr[p] += jnp.dot(w.astype(BF16), vh, preferred_element_type=F32)
                r_scr[h] = r + cs[:, BLK:]

    def worst_total():
        m = r_scr[0]
        for h in range(1, ATTN_HEADS):
            m = jnp.maximum(m, r_scr[h])
        return jnp.max(m)

    block_step(i, True)

    def cond(c):
        j, m = c
        return jnp.logical_and(j >= 0, m > LOG_W_ZERO)

    def body(c):
        j, _ = c
        block_step(j, False)
        return j - 1, worst_total()

    lax.while_loop(cond, body, (i - 1, jnp.float32(0.0)))

    o = jnp.concatenate([acc_scr[p] for p in range(HEAD_PAIRS)], axis=1)
    o_ref[...] = (_rms(o) * g_ref[...]).astype(BF16)


def _attention(q, k, v, g, cmat, batch, seq):
    n = q.shape[0]
    nq = seq // BLK
    qrow = lambda b, i: (b * nq + i, 0)
    kv = lambda b, i: (b, 0)
    fixed = lambda b, i: (0, 0)
    return pl.pallas_call(
        _attn_kernel,
        grid=(batch, nq),
        in_specs=[pl.BlockSpec((BLK, D_ATTN), qrow),
                  pl.BlockSpec((seq, D_ATTN), kv),
                  pl.BlockSpec((seq, D_ATTN), kv),
                  pl.BlockSpec((1, D_ATTN), fixed),
                  pl.BlockSpec((2 * BLK, 2 * BLK), fixed)],
        out_specs=pl.BlockSpec((BLK, D_ATTN), qrow),
        out_shape=jax.ShapeDtypeStruct((n, D_ATTN), BF16),
        scratch_shapes=[pltpu.VMEM((ATTN_HEADS, BLK, LANES), BF16),
                        pltpu.VMEM((ATTN_HEADS, BLK, BLK), F32),
                        pltpu.VMEM((HEAD_PAIRS, BLK, LANES), F32)],
        compiler_params=pltpu.CompilerParams(dimension_semantics=("arbitrary", "arbitrary"),
                                             vmem_limit_bytes=VMEM_LIMIT_BYTES),
        name="attention",
    )(q, k, v, g, cmat)


def _ssm_kernel(u_ref, bre_ref, bim_ref, lbre_ref, lbim_ref, cre_ref, cim_ref, d_ref,
                y_ref, xr_scr, xi_scr, sr_scr, si_scr):
    @pl.when(pl.program_id(0) == 0)
    def _():
        sr_scr[...] = jnp.zeros_like(sr_scr)
        si_scr[...] = jnp.zeros_like(si_scr)

    ub = u_ref[...].astype(BF16)
    for k in range(K_SLICES):
        uk = ub[:, k * MXU_DIM:(k + 1) * MXU_DIM]
        cols = slice(k * STATE_PER_SLICE, (k + 1) * STATE_PER_SLICE)
        xr_scr[:, cols] = jnp.dot(uk, bre_ref[k], preferred_element_type=F32)
        xi_scr[:, cols] = jnp.dot(uk, bim_ref[k], preferred_element_type=F32)

    for c in range(N_STATE // SCAN_COLS):
        cols = slice(c * SCAN_COLS, (c + 1) * SCAN_COLS)
        lr = jnp.broadcast_to(lbre_ref[:, cols], (SUBLANES, SCAN_COLS))
        li = jnp.broadcast_to(lbim_ref[:, cols], (SUBLANES, SCAN_COLS))

        def step(t, carry, cols=cols, lr=lr, li=li):
            sr, si = carry
            rows = pl.ds(pl.multiple_of(t * SUBLANES, SUBLANES), SUBLANES)
            nr = lr * sr - li * si + xr_scr[rows, cols]
            ni = lr * si + li * sr + xi_scr[rows, cols]
            xr_scr[rows, cols] = nr
            xi_scr[rows, cols] = ni
            return nr, ni

        sr, si = lax.fori_loop(0, SCAN_T, step, (sr_scr[:, cols], si_scr[:, cols]), unroll=8)
        sr_scr[:, cols] = sr
        si_scr[:, cols] = si

    for k in range(K_SLICES):
        cols = slice(k * STATE_PER_SLICE, (k + 1) * STATE_PER_SLICE)
        out = slice(k * MXU_DIM, (k + 1) * MXU_DIM)
        y = jnp.dot(xr_scr[:, cols].astype(BF16), cre_ref[k], preferred_element_type=F32)
        y = y - jnp.dot(xi_scr[:, cols].astype(BF16), cim_ref[k], preferred_element_type=F32)
        y_ref[:, out] = y + d_ref[:, out] * u_ref[:, out]


def _ssm_scan(u_tb, bre, bim, lbre, lbim, cre, cim, d):
    n = u_tb.shape[0]
    rows = SCAN_T * SUBLANES
    row = lambda i: (i, 0)
    fixed2 = lambda i: (0, 0)
    fixed3 = lambda i: (0, 0, 0)
    return pl.pallas_call(
        _ssm_kernel,
        grid=(n // rows,),
        in_specs=[pl.BlockSpec((rows, D_SSM), row),
                  pl.BlockSpec((K_SLICES, MXU_DIM, STATE_PER_SLICE), fixed3),
                  pl.BlockSpec((K_SLICES, MXU_DIM, STATE_PER_SLICE), fixed3),
                  pl.BlockSpec((1, N_STATE), fixed2),
                  pl.BlockSpec((1, N_STATE), fixed2),
                  pl.BlockSpec((K_SLICES, STATE_PER_SLICE, MXU_DIM), fixed3),
                  pl.BlockSpec((K_SLICES, STATE_PER_SLICE, MXU_DIM), fixed3),
                  pl.BlockSpec((1, D_SSM), fixed2)],
        out_specs=pl.BlockSpec((rows, D_SSM), row),
        out_shape=jax.ShapeDtypeStruct((n, D_SSM), F32),
        scratch_shapes=[pltpu.VMEM((rows, N_STATE), F32), pltpu.VMEM((rows, N_STATE), F32),
                        pltpu.VMEM((SUBLANES, N_STATE), F32), pltpu.VMEM((SUBLANES, N_STATE), F32)],
        compiler_params=pltpu.CompilerParams(dimension_semantics=("arbitrary",),
                                             vmem_limit_bytes=VMEM_LIMIT_BYTES),
        name="ssm_scan",
    )(u_tb, bre, bim, lbre, lbim, cre, cim, d)


def _post_kernel(x_ref, oa_ref, y_ref, wglu_ref, gs_ref, wout_ref, g2_ref,
                 wg_ref, wu_ref, wd_ref, gf_ref, o_ref):
    yg = jax.nn.gelu(y_ref[...])
    gate = jnp.dot(yg.astype(BF16), wglu_ref[...], preferred_element_type=F32)
    y2 = yg * jax.nn.sigmoid(gate)
    o_ssm = (_rms(y2) * gs_ref[...]).astype(BF16)
    x1 = x_ref[...] + jnp.dot(oa_ref[...], wout_ref[0:D_ATTN, :], preferred_element_type=F32)
    x1 = x1 + jnp.dot(o_ssm, wout_ref[D_ATTN:, :], preferred_element_type=F32)
    h2 = (_rms(x1) * g2_ref[...]).astype(BF16)
    o_ref[...] = x1
    for c in range(D_FF // FF_CHUNK):
        cols = slice(c * FF_CHUNK, (c + 1) * FF_CHUNK)
        gt = jnp.dot(h2, wg_ref[:, cols], preferred_element_type=F32)
        up = jnp.dot(h2, wu_ref[:, cols], preferred_element_type=F32)
        a = (gt * jax.nn.sigmoid(gt) * up).astype(BF16)
        o_ref[...] += jnp.dot(a, wd_ref[cols, :], preferred_element_type=F32)
    o_ref[...] = _rms(o_ref[...]) * gf_ref[...]


def _post(x2, oa, y, wglu, gs, wout, g2, wg, wu, wd, gf):
    n = x2.shape[0]
    row = lambda i: (i, 0)
    fixed = lambda i: (0, 0)
    once = functools.partial(pl.BlockSpec, index_map=fixed, pipeline_mode=pl.Buffered(1))
    return pl.pallas_call(
        _post_kernel,
        grid=(n // TM_POST,),
        in_specs=[pl.BlockSpec((TM_POST, D_MODEL), row),
                  pl.BlockSpec((TM_POST, D_ATTN), row),
                  pl.BlockSpec((TM_POST, D_SSM), row),
                  once((D_SSM, D_SSM)),
                  once((1, D_SSM)),
                  once((D_MODEL, D_MODEL)),
                  once((1, D_MODEL)),
                  once((D_MODEL, D_FF)),
                  once((D_MODEL, D_FF)),
                  once((D_FF, D_MODEL)),
                  once((1, D_MODEL))],
        out_specs=pl.BlockSpec((TM_POST, D_MODEL), row),
        out_shape=jax.ShapeDtypeStruct((n, D_MODEL), F32),
        compiler_params=pltpu.CompilerParams(dimension_semantics=("arbitrary",),
                                             vmem_limit_bytes=VMEM_LIMIT_BYTES),
        name="post",
    )(x2, oa, y, wglu, gs, wout, g2, wg, wu, wd, gf)


def _block_diag_b(bb):
    gl = MXU_DIM // SSM_GROUP
    eye = jnp.eye(gl, dtype=bb.dtype)
    t = bb.reshape(K_SLICES, gl, SSM_GROUP, SSM_STATE)
    m = t[:, :, :, None, :] * eye[None, :, None, :, None]
    return m.reshape(K_SLICES, MXU_DIM, STATE_PER_SLICE).astype(BF16)


def _block_diag_c(c):
    gl = MXU_DIM // SSM_GROUP
    eye = jnp.eye(gl, dtype=c.dtype)
    t = c.reshape(K_SLICES, gl, SSM_GROUP, SSM_STATE).transpose(0, 1, 3, 2)
    m = t[:, :, :, None, :] * eye[None, :, None, :, None]
    return m.reshape(K_SLICES, STATE_PER_SLICE, MXU_DIM).astype(BF16)


def _cumsum_matrix():
    j = jnp.arange(BLK)[:, None]
    s = jnp.arange(BLK)[None, :]
    tri = (j > s).astype(BF16)
    half = jnp.concatenate([tri, jnp.ones((BLK, BLK), BF16)], axis=1)
    return jnp.concatenate([half, half], axis=0)


def kernel(x, norm1_g, w_in, attn_norm_g, lambda_re, lambda_im, log_step, b_re, b_im, c_re, c_im,
           d_skip, w_glu, ssm_norm_g, w_out, norm2_g, w_gate, w_up, w_down, final_norm_g):
    batch, seq, d_model = x.shape
    assert d_model == D_MODEL and norm1_g.shape[0] == 1 and batch == SUBLANES
    n = batch * seq
    x2 = x.reshape(n, D_MODEL)

    q, k, v, u = _in_proj(x2, norm1_g[0][None], w_in[0].astype(BF16))
    oa = _attention(q, k, v, attn_norm_g[0][None], _cumsum_matrix(), batch, seq)

    lb_re, lb_im, bb_re, bb_im = _ssm_prep(
        lambda_re[0], lambda_im[0], log_step[0][:, None],
        b_re[0].transpose(0, 2, 1), b_im[0].transpose(0, 2, 1))
    u_tb = u.reshape(batch, seq, D_SSM).transpose(1, 0, 2).reshape(n, D_SSM)
    y_tb = _ssm_scan(u_tb, _block_diag_b(bb_re), _block_diag_b(bb_im),
                     lb_re.reshape(1, N_STATE), lb_im.reshape(1, N_STATE),
                     _block_diag_c(c_re[0]), _block_diag_c(c_im[0]), d_skip[0].reshape(1, D_SSM))
    y = y_tb.reshape(seq, batch, D_SSM).transpose(1, 0, 2).reshape(n, D_SSM)

    out = _post(x2, oa, y, w_glu[0].astype(BF16), ssm_norm_g[0][None], w_out[0].astype(BF16),
                norm2_g[0][None], w_gate[0].astype(BF16), w_up[0].astype(BF16),
                w_down[0].astype(BF16), final_norm_g[None])
    return out.reshape(batch, seq, D_MODEL)
```

```python
import functools

import jax
import jax.numpy as jnp
from jax import lax
from jax.experimental import pallas as pl
from jax.experimental.pallas import tpu as pltpu

F32 = jnp.float32
BF16 = jnp.bfloat16

D_MODEL = 1024
ATTN_HEADS = 8
HEAD_DIM = 64
D_ATTN = ATTN_HEADS * HEAD_DIM
D_SSM = D_MODEL - D_ATTN
SSM_GROUP = 16
SSM_GROUPS = D_SSM // SSM_GROUP
SSM_STATE = 64
N_STATE = SSM_GROUPS * SSM_STATE
D_FF = 2816
D_IN = 3 * D_ATTN + D_SSM
EPS = 1e-6

LANES = 128
SUBLANES = 8
MXU_DIM = 256
VMEM_LIMIT_BYTES = 56 * 1024 * 1024

TM_PROJ = 512
TM_POST = 512
FF_CHUNK = 256
BLK = 128
HEAD_PAIRS = ATTN_HEADS // 2
SCAN_T = 64
SCAN_COLS = 512
K_SLICES = D_SSM // MXU_DIM
STATE_PER_SLICE = N_STATE // K_SLICES
LOG_W_ZERO = -104.0
MASKED = 1e30


def _rms(x):
    return x * lax.rsqrt(jnp.mean(x * x, axis=-1, keepdims=True) + EPS)


def _ssm_prep_kernel(lre_ref, lim_ref, ls_ref, bre_ref, bim_ref,
                     lbre_ref, lbim_ref, bbre_ref, bbim_ref):
    lam_re = lre_ref[...]
    lam_im = lim_ref[...]
    dt = jnp.exp(ls_ref[...])
    mag = jnp.exp(lam_re * dt)
    ang = lam_im * dt
    lb_re = mag * jnp.cos(ang)
    lb_im = mag * jnp.sin(ang)
    den = lam_re * lam_re + lam_im * lam_im
    num_re = lb_re - 1.0
    f_re = (num_re * lam_re + lb_im * lam_im) / den
    f_im = (lb_im * lam_re - num_re * lam_im) / den
    lbre_ref[...] = lb_re
    lbim_ref[...] = lb_im
    br = bre_ref[...]
    bi = bim_ref[...]
    bbre_ref[...] = f_re[:, None, :] * br - f_im[:, None, :] * bi
    bbim_ref[...] = f_re[:, None, :] * bi + f_im[:, None, :] * br


def _ssm_prep(lambda_re, lambda_im, log_step, b_re_t, b_im_t):
    g, p = lambda_re.shape
    h = b_re_t.shape[1]
    return pl.pallas_call(
        _ssm_prep_kernel,
        out_shape=(jax.ShapeDtypeStruct((g, p), F32), jax.ShapeDtypeStruct((g, p), F32),
                   jax.ShapeDtypeStruct((g, h, p), F32), jax.ShapeDtypeStruct((g, h, p), F32)),
        name="ssm_prep",
    )(lambda_re, lambda_im, log_step, b_re_t, b_im_t)


def _in_proj_kernel(x_ref, g_ref, w_ref, q_ref, k_ref, v_ref, u_ref):
    h = (_rms(x_ref[...]) * g_ref[...]).astype(BF16)
    scale = HEAD_DIM ** -0.5
    q = jnp.dot(h, w_ref[:, 0:D_ATTN], preferred_element_type=F32)
    q_ref[...] = (q * scale).astype(BF16)
    k = jnp.dot(h, w_ref[:, D_ATTN:2 * D_ATTN], preferred_element_type=F32)
    k_ref[...] = k.astype(BF16)
    v = jnp.dot(h, w_ref[:, 2 * D_ATTN:3 * D_ATTN], preferred_element_type=F32)
    v_ref[...] = v.astype(BF16)
    u_ref[...] = jnp.dot(h, w_ref[:, 3 * D_ATTN:], preferred_element_type=F32)


def _in_proj(x2, g, w):
    n = x2.shape[0]
    row = lambda i: (i, 0)
    fixed = lambda i: (0, 0)
    return pl.pallas_call(
        _in_proj_kernel,
        grid=(n // TM_PROJ,),
        in_specs=[pl.BlockSpec((TM_PROJ, D_MODEL), row),
                  pl.BlockSpec((1, D_MODEL), fixed),
                  pl.BlockSpec((D_MODEL, D_IN), fixed)],
        out_specs=[pl.BlockSpec((TM_PROJ, D_ATTN), row)] * 3 + [pl.BlockSpec((TM_PROJ, D_SSM), row)],
        out_shape=[jax.ShapeDtypeStruct((n, D_ATTN), BF16)] * 3 + [jax.ShapeDtypeStruct((n, D_SSM), F32)],
        compiler_params=pltpu.CompilerParams(dimension_semantics=("arbitrary",),
                                             vmem_limit_bytes=VMEM_LIMIT_BYTES),
        name="in_proj",
    )(x2, g, w)


def _attn_kernel(q_ref, k_ref, v_ref, g_ref, cmat_ref, o_ref, qm_scr, z_scr, w_scr, r_scr, acc_scr):
    i = pl.program_id(1)
    row = lax.broadcasted_iota(jnp.int32, (BLK, BLK), 0)
    col = lax.broadcasted_iota(jnp.int32, (BLK, BLK), 1)
    before = col < row
    low_half = col < HEAD_DIM
    cmat = cmat_ref[...]
    nt_dims = (((1,), (1,)), ((), ()))

    def scores(j, p):
        kp = k_ref[pl.ds(pl.multiple_of(j * BLK, BLK), BLK), p * LANES:(p + 1) * LANES]
        return lax.dot_general(qm_scr[p], kp, nt_dims, preferred_element_type=F32)

    def apply_weights(j, p):
        vp = v_ref[pl.ds(pl.multiple_of(j * BLK, BLK), BLK), p * LANES:(p + 1) * LANES]
        vzero = jnp.zeros_like(vp)
        v2 = jnp.concatenate([jnp.where(low_half, vp, vzero), jnp.where(low_half, vzero, vp)], axis=0)
        acc_scr[p] += jnp.dot(w_scr[p], v2, preferred_element_type=F32)

    for p in range(HEAD_PAIRS):
        qp = q_ref[:, p * LANES:(p + 1) * LANES]
        zero = jnp.zeros_like(qp)
        qm_scr[p, 0:BLK] = jnp.where(low_half, qp, zero)
        qm_scr[p, BLK:2 * BLK] = jnp.where(low_half, zero, qp)
        acc_scr[p] = jnp.zeros((BLK, LANES), F32)
        w_scr[p] = jnp.zeros((BLK, 2 * BLK), BF16)
    for h in range(ATTN_HEADS):
        r_scr[h] = jnp.zeros((BLK, BLK), F32)
    for p in range(HEAD_PAIRS):
        z = scores(i, p)
        for hh in range(2):
            rows = slice(hh * BLK, (hh + 1) * BLK)
            z_scr[p, rows] = jnp.where(before, z[rows], -MASKED)

    def body(c):
        j, _ = c
        for p in range(HEAD_PAIRS):
            apply_weights(jnp.minimum(j + 1, i), p)
        log_betas, splits = [], []
        for h in range(ATTN_HEADS):
            z = z_scr[h // 2, (h % 2) * BLK:(h % 2 + 1) * BLK]
            sp = jnp.maximum(z, 0.0) + jnp.log(1.0 + jnp.exp(-jnp.abs(z)))
            log_betas.append(z - sp)
            hi = sp.astype(BF16)
            lo = (sp - hi.astype(F32)).astype(BF16)
            splits.append(jnp.concatenate([hi, lo], axis=1))
        for p in range(HEAD_PAIRS):
            z_scr[p] = scores(jnp.maximum(j - 1, 0), p)
        cs = jnp.dot(jnp.concatenate(splits, axis=0), cmat,
                     preferred_element_type=F32)
        worst = None
        for h in range(ATTN_HEADS):
            csh = cs[h * BLK:(h + 1) * BLK]
            r = r_scr[h]
            r_new = r + csh[:, BLK:]
            r_scr[h] = r_new
            worst = r_new if worst is None else jnp.maximum(worst, r_new)
            w = jnp.exp(log_betas[h] + csh[:, :BLK] + r)
            w_scr[h // 2, :, (h % 2) * BLK:(h % 2 + 1) * BLK] = w.astype(BF16)
        return j - 1, jnp.max(worst)

    def cond(c):
        j, m = c
        return jnp.logical_and(j >= 0, m > LOG_W_ZERO)

    j_next, _ = lax.while_loop(cond, body, (i, jnp.float32(0.0)))
    for p in range(HEAD_PAIRS):
        apply_weights(j_next + 1, p)

    o = jnp.concatenate([acc_scr[p] for p in range(HEAD_PAIRS)], axis=1)
    o_ref[...] = (_rms(o) * g_ref[...]).astype(BF16)


def _attention(q, k, v, g, cmat, batch, seq):
    n = q.shape[0]
    nq = seq // BLK
    qrow = lambda b, i: (b * nq + i, 0)
    kv = lambda b, i: (b, 0)
    fixed = lambda b, i: (0, 0)
    return pl.pallas_call(
        _attn_kernel,
        grid=(batch, nq),
        in_specs=[pl.BlockSpec((BLK, D_ATTN), qrow),
                  pl.BlockSpec((seq, D_ATTN), kv),
                  pl.BlockSpec((seq, D_ATTN), kv),
                  pl.BlockSpec((1, D_ATTN), fixed),
                  pl.BlockSpec((2 * BLK, 2 * BLK), fixed)],
        out_specs=pl.BlockSpec((BLK, D_ATTN), qrow),
        out_shape=jax.ShapeDtypeStruct((n, D_ATTN), BF16),
        scratch_shapes=[pltpu.VMEM((HEAD_PAIRS, 2 * BLK, LANES), BF16),
                        pltpu.VMEM((HEAD_PAIRS, 2 * BLK, BLK), F32),
                        pltpu.VMEM((HEAD_PAIRS, BLK, 2 * BLK), BF16),
                        pltpu.VMEM((ATTN_HEADS, BLK, BLK), F32),
                        pltpu.VMEM((HEAD_PAIRS, BLK, LANES), F32)],
        compiler_params=pltpu.CompilerParams(dimension_semantics=("arbitrary", "arbitrary"),
                                             vmem_limit_bytes=VMEM_LIMIT_BYTES),
        name="attention",
    )(q, k, v, g, cmat)


def _ssm_kernel(u_ref, bre_ref, bim_ref, lbre_ref, lbim_ref, cre_ref, cim_ref, d_ref,
                y_ref, xr_scr, xi_scr, sr_scr, si_scr):
    @pl.when(pl.program_id(0) == 0)
    def _():
        sr_scr[...] = jnp.zeros_like(sr_scr)
        si_scr[...] = jnp.zeros_like(si_scr)

    ub = u_ref[...].astype(BF16)
    for k in range(K_SLICES):
        uk = ub[:, k * MXU_DIM:(k + 1) * MXU_DIM]
        cols = slice(k * STATE_PER_SLICE, (k + 1) * STATE_PER_SLICE)
        xr_scr[:, cols] = jnp.dot(uk, bre_ref[k], preferred_element_type=F32)
        xi_scr[:, cols] = jnp.dot(uk, bim_ref[k], preferred_element_type=F32)

    for c in range(N_STATE // SCAN_COLS):
        cols = slice(c * SCAN_COLS, (c + 1) * SCAN_COLS)
        lr = jnp.broadcast_to(lbre_ref[:, cols], (SUBLANES, SCAN_COLS))
        li = jnp.broadcast_to(lbim_ref[:, cols], (SUBLANES, SCAN_COLS))

        def step(t, carry, cols=cols, lr=lr, li=li):
            sr, si = carry
            rows = pl.ds(pl.multiple_of(t * SUBLANES, SUBLANES), SUBLANES)
            nr = lr * sr - li * si + xr_scr[rows, cols]
            ni = lr * si + li * sr + xi_scr[rows, cols]
            xr_scr[rows, cols] = nr
            xi_scr[rows, cols] = ni
            return nr, ni

        sr, si = lax.fori_loop(0, SCAN_T, step, (sr_scr[:, cols], si_scr[:, cols]), unroll=8)
        sr_scr[:, cols] = sr
        si_scr[:, cols] = si

    for k in range(K_SLICES):
        cols = slice(k * STATE_PER_SLICE, (k + 1) * STATE_PER_SLICE)
        out = slice(k * MXU_DIM, (k + 1) * MXU_DIM)
        y = jnp.dot(xr_scr[:, cols].astype(BF16), cre_ref[k], preferred_element_type=F32)
        y = y - jnp.dot(xi_scr[:, cols].astype(BF16), cim_ref[k], preferred_element_type=F32)
        y_ref[:, out] = y + d_ref[:, out] * u_ref[:, out]


def _ssm_scan(u_tb, bre, bim, lbre, lbim, cre, cim, d):
    n = u_tb.shape[0]
    rows = SCAN_T * SUBLANES
    row = lambda i: (i, 0)
    fixed2 = lambda i: (0, 0)
    fixed3 = lambda i: (0, 0, 0)
    return pl.pallas_call(
        _ssm_kernel,
        grid=(n // rows,),
        in_specs=[pl.BlockSpec((rows, D_SSM), row),
                  pl.BlockSpec((K_SLICES, MXU_DIM, STATE_PER_SLICE), fixed3),
                  pl.BlockSpec((K_SLICES, MXU_DIM, STATE_PER_SLICE), fixed3),
                  pl.BlockSpec((1, N_STATE), fixed2),
                  pl.BlockSpec((1, N_STATE), fixed2),
                  pl.BlockSpec((K_SLICES, STATE_PER_SLICE, MXU_DIM), fixed3),
                  pl.BlockSpec((K_SLICES, STATE_PER_SLICE, MXU_DIM), fixed3),
                  pl.BlockSpec((1, D_SSM), fixed2)],
        out_specs=pl.BlockSpec((rows, D_SSM), row),
        out_shape=jax.ShapeDtypeStruct((n, D_SSM), F32),
        scratch_shapes=[pltpu.VMEM((rows, N_STATE), F32), pltpu.VMEM((rows, N_STATE), F32),
                        pltpu.VMEM((SUBLANES, N_STATE), F32), pltpu.VMEM((SUBLANES, N_STATE), F32)],
        compiler_params=pltpu.CompilerParams(dimension_semantics=("arbitrary",),
                                             vmem_limit_bytes=VMEM_LIMIT_BYTES),
        name="ssm_scan",
    )(u_tb, bre, bim, lbre, lbim, cre, cim, d)


def _post_kernel(x_ref, oa_ref, y_ref, wglu_ref, gs_ref, wout_ref, g2_ref,
                 wg_ref, wu_ref, wd_ref, gf_ref, o_ref):
    yg = jax.nn.gelu(y_ref[...])
    gate = jnp.dot(yg.astype(BF16), wglu_ref[...], preferred_element_type=F32)
    y2 = yg * jax.nn.sigmoid(gate)
    o_ssm = (_rms(y2) * gs_ref[...]).astype(BF16)
    x1 = x_ref[...] + jnp.dot(oa_ref[...], wout_ref[0:D_ATTN, :], preferred_element_type=F32)
    x1 = x1 + jnp.dot(o_ssm, wout_ref[D_ATTN:, :], preferred_element_type=F32)
    h2 = (_rms(x1) * g2_ref[...]).astype(BF16)
    o_ref[...] = x1
    for c in range(D_FF // FF_CHUNK):
        cols = slice(c * FF_CHUNK, (c + 1) * FF_CHUNK)
        gt = jnp.dot(h2, wg_ref[:, cols], preferred_element_type=F32)
        up = jnp.dot(h2, wu_ref[:, cols], preferred_element_type=F32)
        a = (gt * jax.nn.sigmoid(gt) * up).astype(BF16)
        o_ref[...] += jnp.dot(a, wd_ref[cols, :], preferred_element_type=F32)
    o_ref[...] = _rms(o_ref[...]) * gf_ref[...]


def _post(x2, oa, y, wglu, gs, wout, g2, wg, wu, wd, gf):
    n = x2.shape[0]
    row = lambda i: (i, 0)
    fixed = lambda i: (0, 0)
    once = functools.partial(pl.BlockSpec, index_map=fixed, pipeline_mode=pl.Buffered(1))
    return pl.pallas_call(
        _post_kernel,
        grid=(n // TM_POST,),
        in_specs=[pl.BlockSpec((TM_POST, D_MODEL), row),
                  pl.BlockSpec((TM_POST, D_ATTN), row),
                  pl.BlockSpec((TM_POST, D_SSM), row),
                  once((D_SSM, D_SSM)),
                  once((1, D_SSM)),
                  once((D_MODEL, D_MODEL)),
                  once((1, D_MODEL)),
                  once((D_MODEL, D_FF)),
                  once((D_MODEL, D_FF)),
                  once((D_FF, D_MODEL)),
                  once((1, D_MODEL))],
        out_specs=pl.BlockSpec((TM_POST, D_MODEL), row),
        out_shape=jax.ShapeDtypeStruct((n, D_MODEL), F32),
        compiler_params=pltpu.CompilerParams(dimension_semantics=("arbitrary",),
                                             vmem_limit_bytes=VMEM_LIMIT_BYTES),
        name="post",
    )(x2, oa, y, wglu, gs, wout, g2, wg, wu, wd, gf)


def _block_diag_b(bb):
    gl = MXU_DIM // SSM_GROUP
    eye = jnp.eye(gl, dtype=bb.dtype)
    t = bb.reshape(K_SLICES, gl, SSM_GROUP, SSM_STATE)
    m = t[:, :, :, None, :] * eye[None, :, None, :, None]
    return m.reshape(K_SLICES, MXU_DIM, STATE_PER_SLICE).astype(BF16)


def _block_diag_c(c):
    gl = MXU_DIM // SSM_GROUP
    eye = jnp.eye(gl, dtype=c.dtype)
    t = c.reshape(K_SLICES, gl, SSM_GROUP, SSM_STATE).transpose(0, 1, 3, 2)
    m = t[:, :, :, None, :] * eye[None, :, None, :, None]
    return m.reshape(K_SLICES, STATE_PER_SLICE, MXU_DIM).astype(BF16)


def _cumsum_matrix():
    j = jnp.arange(BLK)[:, None]
    s = jnp.arange(BLK)[None, :]
    tri = -(j > s).astype(BF16)
    half = jnp.concatenate([tri, -jnp.ones((BLK, BLK), BF16)], axis=1)
    return jnp.concatenate([half, half], axis=0)


def kernel(x, norm1_g, w_in, attn_norm_g, lambda_re, lambda_im, log_step, b_re, b_im, c_re, c_im,
           d_skip, w_glu, ssm_norm_g, w_out, norm2_g, w_gate, w_up, w_down, final_norm_g):
    batch, seq, d_model = x.shape
    assert d_model == D_MODEL and norm1_g.shape[0] == 1 and batch == SUBLANES
    n = batch * seq
    x2 = x.reshape(n, D_MODEL)

    q, k, v, u = _in_proj(x2, norm1_g[0][None], w_in[0].astype(BF16))
    oa = _attention(q, k, v, attn_norm_g[0][None], _cumsum_matrix(), batch, seq)

    lb_re, lb_im, bb_re, bb_im = _ssm_prep(
        lambda_re[0], lambda_im[0], log_step[0][:, None],
        b_re[0].transpose(0, 2, 1), b_im[0].transpose(0, 2, 1))
    u_tb = u.reshape(batch, seq, D_SSM).transpose(1, 0, 2).reshape(n, D_SSM)
    y_tb = _ssm_scan(u_tb, _block_diag_b(bb_re), _block_diag_b(bb_im),
                     lb_re.reshape(1, N_STATE), lb_im.reshape(1, N_STATE),
                     _block_diag_c(c_re[0]), _block_diag_c(c_im[0]), d_skip[0].reshape(1, D_SSM))
    y = y_tb.reshape(seq, batch, D_SSM).transpose(1, 0, 2).reshape(n, D_SSM)

    out = _post(x2, oa, y, w_glu[0].astype(BF16), ssm_norm_g[0][None], w_out[0].astype(BF16),
                norm2_g[0][None], w_gate[0].astype(BF16), w_up[0].astype(BF16),
                w_down[0].astype(BF16), final_norm_g[None])
    return out.reshape(batch, seq, D_MODEL)
```

```python
import functools

import jax
import jax.numpy as jnp
from jax import lax
from jax.experimental import pallas as pl
from jax.experimental.pallas import tpu as pltpu

F32 = jnp.float32
BF16 = jnp.bfloat16

D_MODEL = 1024
ATTN_HEADS = 8
HEAD_DIM = 64
D_ATTN = ATTN_HEADS * HEAD_DIM
D_SSM = D_MODEL - D_ATTN
SSM_GROUP = 16
SSM_GROUPS = D_SSM // SSM_GROUP
SSM_STATE = 64
N_STATE = SSM_GROUPS * SSM_STATE
D_FF = 2816
D_IN = 3 * D_ATTN + D_SSM
EPS = 1e-6

LANES = 128
SUBLANES = 8
MXU_DIM = 256
VMEM_LIMIT_BYTES = 56 * 1024 * 1024

TM_PROJ = 512
TM_POST = 512
FF_CHUNK = 256
BLK = 128
HEAD_PAIRS = ATTN_HEADS // 2
Q_CHAINS = 2
SCAN_T = 64
SCAN_COLS = 512
K_SLICES = D_SSM // MXU_DIM
STATE_PER_SLICE = N_STATE // K_SLICES
LOG_W_ZERO = -104.0
MASKED = 1e30
LOG2_E = 1.4426950408889634


def _rms(x):
    return x * lax.rsqrt(jnp.mean(x * x, axis=-1, keepdims=True) + EPS)


def _time_major(tiles_per_batch):
    return lambda i: (i % tiles_per_batch, i // tiles_per_batch)


def _ssm_prep_kernel(lre_ref, lim_ref, ls_ref, bre_ref, bim_ref,
                     lbre_ref, lbim_ref, bbre_ref, bbim_ref):
    lam_re = lre_ref[...]
    lam_im = lim_ref[...]
    dt = jnp.exp(ls_ref[...])
    mag = jnp.exp(lam_re * dt)
    ang = lam_im * dt
    lb_re = mag * jnp.cos(ang)
    lb_im = mag * jnp.sin(ang)
    den = lam_re * lam_re + lam_im * lam_im
    num_re = lb_re - 1.0
    f_re = (num_re * lam_re + lb_im * lam_im) / den
    f_im = (lb_im * lam_re - num_re * lam_im) / den
    lbre_ref[...] = lb_re
    lbim_ref[...] = lb_im
    br = bre_ref[...]
    bi = bim_ref[...]
    bbre_ref[...] = f_re[:, None, :] * br - f_im[:, None, :] * bi
    bbim_ref[...] = f_re[:, None, :] * bi + f_im[:, None, :] * br


def _ssm_prep(lambda_re, lambda_im, log_step, b_re_t, b_im_t):
    g, p = lambda_re.shape
    h = b_re_t.shape[1]
    return pl.pallas_call(
        _ssm_prep_kernel,
        out_shape=(jax.ShapeDtypeStruct((g, p), F32), jax.ShapeDtypeStruct((g, p), F32),
                   jax.ShapeDtypeStruct((g, h, p), F32), jax.ShapeDtypeStruct((g, h, p), F32)),
        name="ssm_prep",
    )(lambda_re, lambda_im, log_step, b_re_t, b_im_t)


def _in_proj_kernel(x_ref, g_ref, w_ref, q_ref, k_ref, v_ref, u_ref):
    h = (_rms(x_ref[...]) * g_ref[...]).astype(BF16)
    scale = HEAD_DIM ** -0.5
    q = jnp.dot(h, w_ref[:, 0:D_ATTN], preferred_element_type=F32)
    q_ref[...] = (q * scale).astype(BF16)
    k = jnp.dot(h, w_ref[:, D_ATTN:2 * D_ATTN], preferred_element_type=F32)
    k_ref[...] = k.astype(BF16)
    v = jnp.dot(h, w_ref[:, 2 * D_ATTN:3 * D_ATTN], preferred_element_type=F32)
    v_ref[...] = v.astype(BF16)
    u_ref[...] = jnp.dot(h, w_ref[:, 3 * D_ATTN:], preferred_element_type=F32)


def _in_proj(x2, g, w, batch, seq):
    n = x2.shape[0]
    row = lambda i: (i, 0)
    fixed = lambda i: (0, 0)
    return pl.pallas_call(
        _in_proj_kernel,
        grid=(n // TM_PROJ,),
        in_specs=[pl.BlockSpec((TM_PROJ, D_MODEL), row),
                  pl.BlockSpec((1, D_MODEL), fixed),
                  pl.BlockSpec((D_MODEL, D_IN), fixed)],
        out_specs=[pl.BlockSpec((TM_PROJ, D_ATTN), row)] * 3
        + [pl.BlockSpec((TM_PROJ, D_SSM), _time_major(seq // TM_PROJ))],
        out_shape=[jax.ShapeDtypeStruct((n, D_ATTN), BF16)] * 3
        + [jax.ShapeDtypeStruct((seq, batch * D_SSM), F32)],
        compiler_params=pltpu.CompilerParams(dimension_semantics=("arbitrary",),
                                             vmem_limit_bytes=VMEM_LIMIT_BYTES),
        name="in_proj",
    )(x2, g, w)


def _attn_kernel(q_ref, k_ref, v_ref, g_ref, cmat_ref, o_ref, qm_scr, z_scr, w_scr, r_scr, acc_scr):
    step = pl.program_id(1)
    row = lax.broadcasted_iota(jnp.int32, (BLK, BLK), 0)
    col = lax.broadcasted_iota(jnp.int32, (BLK, BLK), 1)
    before = col < row
    low_half = col < HEAD_DIM
    high_half = jnp.logical_not(low_half)
    cmat = cmat_ref[...]
    nt_dims = (((1,), (1,)), ((), ()))
    first = [step * Q_CHAINS + c for c in range(Q_CHAINS)]

    def key_rows(j, c):
        return pl.ds(pl.multiple_of(jnp.clip(j, 0, first[c]) * BLK, BLK), BLK)

    def scores(j, c, p):
        kp = k_ref[key_rows(j, c), p * LANES:(p + 1) * LANES]
        return lax.dot_general(qm_scr[c * HEAD_PAIRS + p], kp, nt_dims,
                               preferred_element_type=F32)

    def apply_weights(j, c, p):
        valid = j >= 0
        vp = v_ref[key_rows(j, c), p * LANES:(p + 1) * LANES]
        vzero = jnp.zeros_like(vp)
        v2 = jnp.concatenate([jnp.where(jnp.logical_and(low_half, valid), vp, vzero),
                              jnp.where(jnp.logical_and(high_half, valid), vp, vzero)], axis=0)
        u = c * HEAD_PAIRS + p
        acc_scr[u] += jnp.dot(w_scr[u], v2, preferred_element_type=F32)

    for c in range(Q_CHAINS):
        for p in range(HEAD_PAIRS):
            u = c * HEAD_PAIRS + p
            qp = q_ref[c * BLK:(c + 1) * BLK, p * LANES:(p + 1) * LANES]
            zero = jnp.zeros_like(qp)
            qm_scr[u, 0:BLK] = jnp.where(low_half, qp, zero)
            qm_scr[u, BLK:2 * BLK] = jnp.where(low_half, zero, qp)
            acc_scr[u] = jnp.zeros((BLK, LANES), F32)
            w_scr[u] = jnp.zeros((BLK, 2 * BLK), BF16)
        for h in range(ATTN_HEADS):
            r_scr[c * ATTN_HEADS + h] = jnp.zeros((BLK, BLK), F32)
    for c in range(Q_CHAINS):
        for p in range(HEAD_PAIRS):
            z = scores(first[c], c, p)
            for hh in range(2):
                rows = slice(hh * BLK, (hh + 1) * BLK)
                z_scr[c * HEAD_PAIRS + p, rows] = jnp.where(before, z[rows], -MASKED)

    units = [(c, h) for c in range(Q_CHAINS) for h in range(ATTN_HEADS)]

    def body(carry):
        t, _ = carry
        for c in range(Q_CHAINS):
            for p in range(HEAD_PAIRS):
                apply_weights(first[c] - t + 1, c, p)
        splits = []
        for c, h in units:
            z = z_scr[c * HEAD_PAIRS + h // 2, (h % 2) * BLK:(h % 2 + 1) * BLK]
            sp = jnp.maximum(z, 0.0) + jnp.log(1.0 + jnp.exp2(jnp.abs(z) * -LOG2_E))
            hi = sp.astype(BF16)
            lo = (sp - hi.astype(F32)).astype(BF16)
            splits.append(jnp.concatenate([hi, lo], axis=1))
        cs = jnp.dot(jnp.concatenate(splits, axis=0), cmat,
                     preferred_element_type=F32)
        worst = [None] * Q_CHAINS
        for n, (c, h) in enumerate(units):
            csn = cs[n * BLK:(n + 1) * BLK]
            r = r_scr[n]
            r_new = r + csn[:, BLK:]
            r_scr[n] = r_new
            worst[c] = r_new if worst[c] is None else jnp.maximum(worst[c], r_new)
            z = z_scr[c * HEAD_PAIRS + h // 2, (h % 2) * BLK:(h % 2 + 1) * BLK]
            w = jnp.exp2((z + csn[:, :BLK] + r) * LOG2_E)
            w_scr[c * HEAD_PAIRS + h // 2, :, (h % 2) * BLK:(h % 2 + 1) * BLK] = w.astype(BF16)
        for c in range(Q_CHAINS):
            for p in range(HEAD_PAIRS):
                z_scr[c * HEAD_PAIRS + p] = scores(first[c] - t - 1, c, p)
        m = None
        for c in range(Q_CHAINS):
            mc = jnp.where(first[c] - t - 1 >= 0, jnp.max(worst[c]), -jnp.inf)
            m = mc if m is None else jnp.maximum(m, mc)
        return t + 1, m

    def cond(carry):
        return carry[1] > LOG_W_ZERO

    t_end, _ = lax.while_loop(cond, body, (jnp.int32(0), jnp.float32(0.0)))
    for c in range(Q_CHAINS):
        for p in range(HEAD_PAIRS):
            apply_weights(first[c] - t_end + 1, c, p)

    for c in range(Q_CHAINS):
        o = jnp.concatenate([acc_scr[c * HEAD_PAIRS + p] for p in range(HEAD_PAIRS)], axis=1)
        o_ref[c * BLK:(c + 1) * BLK, :] = (_rms(o) * g_ref[...]).astype(BF16)


def _attention(q, k, v, g, cmat, batch, seq):
    n = q.shape[0]
    rows = Q_CHAINS * BLK
    steps = seq // rows
    qrow = lambda b, i: (b * steps + i, 0)
    kv = lambda b, i: (b, 0)
    fixed = lambda b, i: (0, 0)
    pairs = Q_CHAINS * HEAD_PAIRS
    return pl.pallas_call(
        _attn_kernel,
        grid=(batch, steps),
        in_specs=[pl.BlockSpec((rows, D_ATTN), qrow),
                  pl.BlockSpec((seq, D_ATTN), kv),
                  pl.BlockSpec((seq, D_ATTN), kv),
                  pl.BlockSpec((1, D_ATTN), fixed),
                  pl.BlockSpec((2 * BLK, 2 * BLK), fixed)],
        out_specs=pl.BlockSpec((rows, D_ATTN), qrow),
        out_shape=jax.ShapeDtypeStruct((n, D_ATTN), BF16),
        scratch_shapes=[pltpu.VMEM((pairs, 2 * BLK, LANES), BF16),
                        pltpu.VMEM((pairs, 2 * BLK, BLK), F32),
                        pltpu.VMEM((pairs, BLK, 2 * BLK), BF16),
                        pltpu.VMEM((Q_CHAINS * ATTN_HEADS, BLK, BLK), F32),
                        pltpu.VMEM((pairs, BLK, LANES), F32)],
        compiler_params=pltpu.CompilerParams(dimension_semantics=("arbitrary", "arbitrary"),
                                             vmem_limit_bytes=VMEM_LIMIT_BYTES),
        name="attention",
    )(q, k, v, g, cmat)


def _ssm_kernel(u_ref, bre_ref, bim_ref, lbre_ref, lbim_ref, cre_ref, cim_ref, d_ref,
                y_ref, xr_scr, xi_scr, sr_scr, si_scr):
    @pl.when(pl.program_id(0) == 0)
    def _():
        sr_scr[...] = jnp.zeros_like(sr_scr)
        si_scr[...] = jnp.zeros_like(si_scr)

    ub = u_ref[...].astype(BF16)
    for k in range(K_SLICES):
        uk = ub[:, k * MXU_DIM:(k + 1) * MXU_DIM]
        cols = slice(k * STATE_PER_SLICE, (k + 1) * STATE_PER_SLICE)
        xr_scr[:, cols] = jnp.dot(uk, bre_ref[k], preferred_element_type=F32)
        xi_scr[:, cols] = jnp.dot(uk, bim_ref[k], preferred_element_type=F32)

    for c in range(N_STATE // SCAN_COLS):
        cols = slice(c * SCAN_COLS, (c + 1) * SCAN_COLS)
        lr = jnp.broadcast_to(lbre_ref[:, cols], (SUBLANES, SCAN_COLS))
        li = jnp.broadcast_to(lbim_ref[:, cols], (SUBLANES, SCAN_COLS))

        def step(t, carry, cols=cols, lr=lr, li=li):
            sr, si = carry
            rows = pl.ds(pl.multiple_of(t * SUBLANES, SUBLANES), SUBLANES)
            nr = lr * sr - li * si + xr_scr[rows, cols]
            ni = lr * si + li * sr + xi_scr[rows, cols]
            xr_scr[rows, cols] = nr
            xi_scr[rows, cols] = ni
            return nr, ni

        sr, si = lax.fori_loop(0, SCAN_T, step, (sr_scr[:, cols], si_scr[:, cols]), unroll=8)
        sr_scr[:, cols] = sr
        si_scr[:, cols] = si

    for k in range(K_SLICES):
        cols = slice(k * STATE_PER_SLICE, (k + 1) * STATE_PER_SLICE)
        out = slice(k * MXU_DIM, (k + 1) * MXU_DIM)
        y = jnp.dot(xr_scr[:, cols].astype(BF16), cre_ref[k], preferred_element_type=F32)
        y = y - jnp.dot(xi_scr[:, cols].astype(BF16), cim_ref[k], preferred_element_type=F32)
        y_ref[:, out] = y + d_ref[:, out] * u_ref[:, out]


def _ssm_scan(u_tb, bre, bim, lbre, lbim, cre, cim, d):
    n = u_tb.shape[0]
    rows = SCAN_T * SUBLANES
    row = lambda i: (i, 0)
    fixed2 = lambda i: (0, 0)
    fixed3 = lambda i: (0, 0, 0)
    return pl.pallas_call(
        _ssm_kernel,
        grid=(n // rows,),
        in_specs=[pl.BlockSpec((rows, D_SSM), row),
                  pl.BlockSpec((K_SLICES, MXU_DIM, STATE_PER_SLICE), fixed3),
                  pl.BlockSpec((K_SLICES, MXU_DIM, STATE_PER_SLICE), fixed3),
                  pl.BlockSpec((1, N_STATE), fixed2),
                  pl.BlockSpec((1, N_STATE), fixed2),
                  pl.BlockSpec((K_SLICES, STATE_PER_SLICE, MXU_DIM), fixed3),
                  pl.BlockSpec((K_SLICES, STATE_PER_SLICE, MXU_DIM), fixed3),
                  pl.BlockSpec((1, D_SSM), fixed2)],
        out_specs=pl.BlockSpec((rows, D_SSM), row),
        out_shape=jax.ShapeDtypeStruct((n, D_SSM), F32),
        scratch_shapes=[pltpu.VMEM((rows, N_STATE), F32), pltpu.VMEM((rows, N_STATE), F32),
                        pltpu.VMEM((SUBLANES, N_STATE), F32), pltpu.VMEM((SUBLANES, N_STATE), F32)],
        compiler_params=pltpu.CompilerParams(dimension_semantics=("arbitrary",),
                                             vmem_limit_bytes=VMEM_LIMIT_BYTES),
        name="ssm_scan",
    )(u_tb, bre, bim, lbre, lbim, cre, cim, d)


def _post_kernel(x_ref, oa_ref, y_ref, wglu_ref, gs_ref, wout_ref, g2_ref,
                 wg_ref, wu_ref, wd_ref, gf_ref, o_ref):
    yg = jax.nn.gelu(y_ref[...])
    gate = jnp.dot(yg.astype(BF16), wglu_ref[...], preferred_element_type=F32)
    y2 = yg * jax.nn.sigmoid(gate)
    o_ssm = (_rms(y2) * gs_ref[...]).astype(BF16)
    x1 = x_ref[...] + jnp.dot(oa_ref[...], wout_ref[0:D_ATTN, :], preferred_element_type=F32)
    x1 = x1 + jnp.dot(o_ssm, wout_ref[D_ATTN:, :], preferred_element_type=F32)
    h2 = (_rms(x1) * g2_ref[...]).astype(BF16)
    o_ref[...] = x1
    for c in range(D_FF // FF_CHUNK):
        cols = slice(c * FF_CHUNK, (c + 1) * FF_CHUNK)
        gt = jnp.dot(h2, wg_ref[:, cols], preferred_element_type=F32)
        up = jnp.dot(h2, wu_ref[:, cols], preferred_element_type=F32)
        a = (gt * jax.nn.sigmoid(gt) * up).astype(BF16)
        o_ref[...] += jnp.dot(a, wd_ref[cols, :], preferred_element_type=F32)
    o_ref[...] = _rms(o_ref[...]) * gf_ref[...]


def _post(x2, oa, y_tm, wglu, gs, wout, g2, wg, wu, wd, gf, seq):
    n = x2.shape[0]
    row = lambda i: (i, 0)
    fixed = lambda i: (0, 0)
    once = functools.partial(pl.BlockSpec, index_map=fixed, pipeline_mode=pl.Buffered(1))
    return pl.pallas_call(
        _post_kernel,
        grid=(n // TM_POST,),
        in_specs=[pl.BlockSpec((TM_POST, D_MODEL), row),
                  pl.BlockSpec((TM_POST, D_ATTN), row),
                  pl.BlockSpec((TM_POST, D_SSM), _time_major(seq // TM_POST)),
                  once((D_SSM, D_SSM)),
                  once((1, D_SSM)),
                  once((D_MODEL, D_MODEL)),
                  once((1, D_MODEL)),
                  once((D_MODEL, D_FF)),
                  once((D_MODEL, D_FF)),
                  once((D_FF, D_MODEL)),
                  once((1, D_MODEL))],
        out_specs=pl.BlockSpec((TM_POST, D_MODEL), row),
        out_shape=jax.ShapeDtypeStruct((n, D_MODEL), F32),
        compiler_params=pltpu.CompilerParams(dimension_semantics=("arbitrary",),
                                             vmem_limit_bytes=VMEM_LIMIT_BYTES),
        name="post",
    )(x2, oa, y_tm, wglu, gs, wout, g2, wg, wu, wd, gf)


def _block_diag_b(bb):
    gl = MXU_DIM // SSM_GROUP
    eye = jnp.eye(gl, dtype=bb.dtype)
    t = bb.reshape(K_SLICES, gl, SSM_GROUP, SSM_STATE)
    m = t[:, :, :, None, :] * eye[None, :, None, :, None]
    return m.reshape(K_SLICES, MXU_DIM, STATE_PER_SLICE).astype(BF16)


def _block_diag_c(c):
    gl = MXU_DIM // SSM_GROUP
    eye = jnp.eye(gl, dtype=c.dtype)
    t = c.reshape(K_SLICES, gl, SSM_GROUP, SSM_STATE).transpose(0, 1, 3, 2)
    m = t[:, :, :, None, :] * eye[None, :, None, :, None]
    return m.reshape(K_SLICES, STATE_PER_SLICE, MXU_DIM).astype(BF16)


def _cumsum_matrix():
    j = jnp.arange(BLK)[:, None]
    s = jnp.arange(BLK)[None, :]
    tri = -(j >= s).astype(BF16)
    half = jnp.concatenate([tri, -jnp.ones((BLK, BLK), BF16)], axis=1)
    return jnp.concatenate([half, half], axis=0)


def kernel(x, norm1_g, w_in, attn_norm_g, lambda_re, lambda_im, log_step, b_re, b_im, c_re, c_im,
           d_skip, w_glu, ssm_norm_g, w_out, norm2_g, w_gate, w_up, w_down, final_norm_g):
    batch, seq, d_model = x.shape
    assert d_model == D_MODEL and norm1_g.shape[0] == 1 and batch == SUBLANES
    n = batch * seq
    x2 = x.reshape(n, D_MODEL)

    q, k, v, u_tm = _in_proj(x2, norm1_g[0][None], w_in[0].astype(BF16), batch, seq)
    oa = _attention(q, k, v, attn_norm_g[0][None], _cumsum_matrix(), batch, seq)

    lb_re, lb_im, bb_re, bb_im = _ssm_prep(
        lambda_re[0], lambda_im[0], log_step[0][:, None],
        b_re[0].transpose(0, 2, 1), b_im[0].transpose(0, 2, 1))
    y_tb = _ssm_scan(u_tm.reshape(n, D_SSM), _block_diag_b(bb_re), _block_diag_b(bb_im),
                     lb_re.reshape(1, N_STATE), lb_im.reshape(1, N_STATE),
                     _block_diag_c(c_re[0]), _block_diag_c(c_im[0]), d_skip[0].reshape(1, D_SSM))

    out = _post(x2, oa, y_tb.reshape(seq, batch * D_SSM), w_glu[0].astype(BF16), ssm_norm_g[0][None],
                w_out[0].astype(BF16), norm2_g[0][None], w_gate[0].astype(BF16),
                w_up[0].astype(BF16), w_down[0].astype(BF16), final_norm_g[None], seq)
    return out.reshape(batch, seq, D_MODEL)
```

```python
import functools

import jax
import jax.numpy as jnp
from jax import lax
from jax.experimental import pallas as pl
from jax.experimental.pallas import tpu as pltpu

F32 = jnp.float32
BF16 = jnp.bfloat16

D_MODEL = 1024
ATTN_HEADS = 8
HEAD_DIM = 64
D_ATTN = ATTN_HEADS * HEAD_DIM
D_SSM = D_MODEL - D_ATTN
SSM_GROUP = 16
SSM_GROUPS = D_SSM // SSM_GROUP
SSM_STATE = 64
N_STATE = SSM_GROUPS * SSM_STATE
D_FF = 2816
D_IN = 3 * D_ATTN + D_SSM
EPS = 1e-6

LANES = 128
SUBLANES = 8
MXU_DIM = 256
VMEM_LIMIT_BYTES = 56 * 1024 * 1024

TM_PROJ = 512
TM_POST = 512
FF_CHUNK = 256
BLK = 128
HEAD_PAIRS = ATTN_HEADS // 2
Q_CHAINS = 2
SCAN_T = 64
SCAN_COLS = 512
K_SLICES = D_SSM // MXU_DIM
STATE_PER_SLICE = N_STATE // K_SLICES
SSM_SLABS = D_SSM // LANES
LOG_W_ZERO = -104.0
MASKED = 1e30
LOG2_E = 1.4426950408889634


def _rms(x):
    return x * lax.rsqrt(jnp.mean(x * x, axis=-1, keepdims=True) + EPS)


def _ssm_prep_kernel(lre_ref, lim_ref, ls_ref, bre_ref, bim_ref,
                     lbre_ref, lbim_ref, bbre_ref, bbim_ref):
    lam_re = lre_ref[...]
    lam_im = lim_ref[...]
    dt = jnp.exp(ls_ref[...])
    mag = jnp.exp(lam_re * dt)
    ang = lam_im * dt
    lb_re = mag * jnp.cos(ang)
    lb_im = mag * jnp.sin(ang)
    den = lam_re * lam_re + lam_im * lam_im
    num_re = lb_re - 1.0
    f_re = (num_re * lam_re + lb_im * lam_im) / den
    f_im = (lb_im * lam_re - num_re * lam_im) / den
    lbre_ref[...] = lb_re
    lbim_ref[...] = lb_im
    br = bre_ref[...]
    bi = bim_ref[...]
    bbre_ref[...] = f_re[:, None, :] * br - f_im[:, None, :] * bi
    bbim_ref[...] = f_re[:, None, :] * bi + f_im[:, None, :] * br


def _ssm_prep(lambda_re, lambda_im, log_step, b_re_t, b_im_t):
    g, p = lambda_re.shape
    h = b_re_t.shape[1]
    return pl.pallas_call(
        _ssm_prep_kernel,
        out_shape=(jax.ShapeDtypeStruct((g, p), F32), jax.ShapeDtypeStruct((g, p), F32),
                   jax.ShapeDtypeStruct((g, h, p), F32), jax.ShapeDtypeStruct((g, h, p), F32)),
        name="ssm_prep",
    )(lambda_re, lambda_im, log_step, b_re_t, b_im_t)


def _in_proj_kernel(x_ref, g_ref, w_ref, q_ref, k_ref, v_ref, u_ref):
    h = (_rms(x_ref[...]) * g_ref[...]).astype(BF16)
    scale = HEAD_DIM ** -0.5
    q = jnp.dot(h, w_ref[:, 0:D_ATTN], preferred_element_type=F32)
    q_ref[...] = (q * scale).astype(BF16)
    k = jnp.dot(h, w_ref[:, D_ATTN:2 * D_ATTN], preferred_element_type=F32)
    k_ref[...] = k.astype(BF16)
    v = jnp.dot(h, w_ref[:, 2 * D_ATTN:3 * D_ATTN], preferred_element_type=F32)
    v_ref[...] = v.astype(BF16)
    u_ref[...] = jnp.dot(h, w_ref[:, 3 * D_ATTN:], preferred_element_type=F32)


def _in_proj(x2, g, w):
    n = x2.shape[0]
    row = lambda i: (i, 0)
    fixed = lambda i: (0, 0)
    return pl.pallas_call(
        _in_proj_kernel,
        grid=(n // TM_PROJ,),
        in_specs=[pl.BlockSpec((TM_PROJ, D_MODEL), row),
                  pl.BlockSpec((1, D_MODEL), fixed),
                  pl.BlockSpec((D_MODEL, D_IN), fixed)],
        out_specs=[pl.BlockSpec((TM_PROJ, D_ATTN), row)] * 3 + [pl.BlockSpec((TM_PROJ, D_SSM), row)],
        out_shape=[jax.ShapeDtypeStruct((n, D_ATTN), BF16)] * 3 + [jax.ShapeDtypeStruct((n, D_SSM), F32)],
        compiler_params=pltpu.CompilerParams(dimension_semantics=("arbitrary",),
                                             vmem_limit_bytes=VMEM_LIMIT_BYTES),
        name="in_proj",
    )(x2, g, w)


def _attn_kernel(q_ref, k_ref, v_ref, g_ref, cmat_ref, o_ref, qm_scr, z_scr, w_scr, r_scr, acc_scr):
    step = pl.program_id(1)
    row = lax.broadcasted_iota(jnp.int32, (BLK, BLK), 0)
    col = lax.broadcasted_iota(jnp.int32, (BLK, BLK), 1)
    before = col < row
    low_half = col < HEAD_DIM
    high_half = jnp.logical_not(low_half)
    cmat = cmat_ref[...]
    nt_dims = (((1,), (1,)), ((), ()))
    first = [step * Q_CHAINS + c for c in range(Q_CHAINS)]

    def key_rows(j, c):
        return pl.ds(pl.multiple_of(jnp.clip(j, 0, first[c]) * BLK, BLK), BLK)

    def scores(j, c, p):
        kp = k_ref[key_rows(j, c), p * LANES:(p + 1) * LANES]
        return lax.dot_general(qm_scr[c * HEAD_PAIRS + p], kp, nt_dims,
                               preferred_element_type=F32)

    def apply_weights(j, c, p):
        valid = j >= 0
        vp = v_ref[key_rows(j, c), p * LANES:(p + 1) * LANES]
        vzero = jnp.zeros_like(vp)
        v2 = jnp.concatenate([jnp.where(jnp.logical_and(low_half, valid), vp, vzero),
                              jnp.where(jnp.logical_and(high_half, valid), vp, vzero)], axis=0)
        u = c * HEAD_PAIRS + p
        acc_scr[u] += jnp.dot(w_scr[u], v2, preferred_element_type=F32)

    for c in range(Q_CHAINS):
        for p in range(HEAD_PAIRS):
            u = c * HEAD_PAIRS + p
            qp = q_ref[c * BLK:(c + 1) * BLK, p * LANES:(p + 1) * LANES]
            zero = jnp.zeros_like(qp)
            qm_scr[u, 0:BLK] = jnp.where(low_half, qp, zero)
            qm_scr[u, BLK:2 * BLK] = jnp.where(low_half, zero, qp)
            acc_scr[u] = jnp.zeros((BLK, LANES), F32)
            w_scr[u] = jnp.zeros((BLK, 2 * BLK), BF16)
        for h in range(ATTN_HEADS):
            r_scr[c * ATTN_HEADS + h] = jnp.zeros((BLK, BLK), F32)
    for c in range(Q_CHAINS):
        for p in range(HEAD_PAIRS):
            z = scores(first[c], c, p)
            for hh in range(2):
                rows = slice(hh * BLK, (hh + 1) * BLK)
                z_scr[c * HEAD_PAIRS + p, rows] = jnp.where(before, z[rows], -MASKED)

    units = [(c, h) for c in range(Q_CHAINS) for h in range(ATTN_HEADS)]

    def body(carry):
        t, _ = carry
        for c in range(Q_CHAINS):
            for p in range(HEAD_PAIRS):
                apply_weights(first[c] - t + 1, c, p)
        splits = []
        for c, h in units:
            z = z_scr[c * HEAD_PAIRS + h // 2, (h % 2) * BLK:(h % 2 + 1) * BLK]
            sp = jnp.maximum(z, 0.0) + jnp.log(1.0 + jnp.exp2(jnp.abs(z) * -LOG2_E))
            hi = sp.astype(BF16)
            lo = (sp - hi.astype(F32)).astype(BF16)
            splits.append(jnp.concatenate([hi, lo], axis=1))
        cs = jnp.dot(jnp.concatenate(splits, axis=0), cmat,
                     preferred_element_type=F32)
        worst = [None] * Q_CHAINS
        for n, (c, h) in enumerate(units):
            csn = cs[n * BLK:(n + 1) * BLK]
            r = r_scr[n]
            r_new = r + csn[:, BLK:]
            r_scr[n] = r_new
            worst[c] = r_new if worst[c] is None else jnp.maximum(worst[c], r_new)
            z = z_scr[c * HEAD_PAIRS + h // 2, (h % 2) * BLK:(h % 2 + 1) * BLK]
            w = jnp.exp2((z + csn[:, :BLK] + r) * LOG2_E)
            w_scr[c * HEAD_PAIRS + h // 2, :, (h % 2) * BLK:(h % 2 + 1) * BLK] = w.astype(BF16)
        for c in range(Q_CHAINS):
            for p in range(HEAD_PAIRS):
                z_scr[c * HEAD_PAIRS + p] = scores(first[c] - t - 1, c, p)
        m = None
        for c in range(Q_CHAINS):
            mc = jnp.where(first[c] - t - 1 >= 0, jnp.max(worst[c]), -jnp.inf)
            m = mc if m is None else jnp.maximum(m, mc)
        return t + 1, m

    def cond(carry):
        return carry[1] > LOG_W_ZERO

    t_end, _ = lax.while_loop(cond, body, (jnp.int32(0), jnp.float32(0.0)))
    for c in range(Q_CHAINS):
        for p in range(HEAD_PAIRS):
            apply_weights(first[c] - t_end + 1, c, p)

    for c in range(Q_CHAINS):
        o = jnp.concatenate([acc_scr[c * HEAD_PAIRS + p] for p in range(HEAD_PAIRS)], axis=1)
        o_ref[c * BLK:(c + 1) * BLK, :] = (_rms(o) * g_ref[...]).astype(BF16)


def _attention(q, k, v, g, cmat, batch, seq):
    n = q.shape[0]
    rows = Q_CHAINS * BLK
    steps = seq // rows
    qrow = lambda b, i: (b * steps + i, 0)
    kv = lambda b, i: (b, 0)
    fixed = lambda b, i: (0, 0)
    pairs = Q_CHAINS * HEAD_PAIRS
    return pl.pallas_call(
        _attn_kernel,
        grid=(batch, steps),
        in_specs=[pl.BlockSpec((rows, D_ATTN), qrow),
                  pl.BlockSpec((seq, D_ATTN), kv),
                  pl.BlockSpec((seq, D_ATTN), kv),
                  pl.BlockSpec((1, D_ATTN), fixed),
                  pl.BlockSpec((2 * BLK, 2 * BLK), fixed)],
        out_specs=pl.BlockSpec((rows, D_ATTN), qrow),
        out_shape=jax.ShapeDtypeStruct((n, D_ATTN), BF16),
        scratch_shapes=[pltpu.VMEM((pairs, 2 * BLK, LANES), BF16),
                        pltpu.VMEM((pairs, 2 * BLK, BLK), F32),
                        pltpu.VMEM((pairs, BLK, 2 * BLK), BF16),
                        pltpu.VMEM((Q_CHAINS * ATTN_HEADS, BLK, BLK), F32),
                        pltpu.VMEM((pairs, BLK, LANES), F32)],
        compiler_params=pltpu.CompilerParams(dimension_semantics=("arbitrary", "arbitrary"),
                                             vmem_limit_bytes=VMEM_LIMIT_BYTES),
        name="attention",
    )(q, k, v, g, cmat)


def _ssm_kernel(u_ref, bre_ref, bim_ref, lbre_ref, lbim_ref, cre_ref, cim_ref, d_ref,
                y_ref, utb_scr, ytb_scr, xr_scr, xi_scr, sr_scr, si_scr):
    @pl.when(pl.program_id(0) == 0)
    def _():
        sr_scr[...] = jnp.zeros_like(sr_scr)
        si_scr[...] = jnp.zeros_like(si_scr)

    for b in range(SUBLANES):
        for s in range(SSM_SLABS):
            utb_scr[s, pl.ds(b, SCAN_T, stride=SUBLANES), :] = u_ref[b, :, s * LANES:(s + 1) * LANES]

    slabs_per_slice = MXU_DIM // LANES
    for k in range(K_SLICES):
        uk = jnp.concatenate([utb_scr[k * slabs_per_slice + s] for s in range(slabs_per_slice)],
                             axis=1).astype(BF16)
        cols = slice(k * STATE_PER_SLICE, (k + 1) * STATE_PER_SLICE)
        xr_scr[:, cols] = jnp.dot(uk, bre_ref[k], preferred_element_type=F32)
        xi_scr[:, cols] = jnp.dot(uk, bim_ref[k], preferred_element_type=F32)

    for c in range(N_STATE // SCAN_COLS):
        cols = slice(c * SCAN_COLS, (c + 1) * SCAN_COLS)
        lr = jnp.broadcast_to(lbre_ref[:, cols], (SUBLANES, SCAN_COLS))
        li = jnp.broadcast_to(lbim_ref[:, cols], (SUBLANES, SCAN_COLS))

        def step(t, carry, cols=cols, lr=lr, li=li):
            sr, si = carry
            rows = pl.ds(pl.multiple_of(t * SUBLANES, SUBLANES), SUBLANES)
            nr = lr * sr - li * si + xr_scr[rows, cols]
            ni = lr * si + li * sr + xi_scr[rows, cols]
            xr_scr[rows, cols] = nr
            xi_scr[rows, cols] = ni
            return nr, ni

        sr, si = lax.fori_loop(0, SCAN_T, step, (sr_scr[:, cols], si_scr[:, cols]), unroll=8)
        sr_scr[:, cols] = sr
        si_scr[:, cols] = si

    for k in range(K_SLICES):
        cols = slice(k * STATE_PER_SLICE, (k + 1) * STATE_PER_SLICE)
        y = jnp.dot(xr_scr[:, cols].astype(BF16), cre_ref[k], preferred_element_type=F32)
        y = y - jnp.dot(xi_scr[:, cols].astype(BF16), cim_ref[k], preferred_element_type=F32)
        for s in range(slabs_per_slice):
            slab = k * slabs_per_slice + s
            lanes = slice(slab * LANES, (slab + 1) * LANES)
            ytb_scr[slab] = y[:, s * LANES:(s + 1) * LANES] + d_ref[:, lanes] * utb_scr[slab]

    for b in range(SUBLANES):
        for s in range(SSM_SLABS):
            y_ref[b, :, s * LANES:(s + 1) * LANES] = ytb_scr[s, pl.ds(b, SCAN_T, stride=SUBLANES), :]


def _ssm_scan(u3, bre, bim, lbre, lbim, cre, cim, d):
    batch, seq, _ = u3.shape
    rows = SCAN_T * batch
    chunk = lambda i: (0, i, 0)
    fixed2 = lambda i: (0, 0)
    fixed3 = lambda i: (0, 0, 0)
    return pl.pallas_call(
        _ssm_kernel,
        grid=(seq // SCAN_T,),
        in_specs=[pl.BlockSpec((batch, SCAN_T, D_SSM), chunk),
                  pl.BlockSpec((K_SLICES, MXU_DIM, STATE_PER_SLICE), fixed3),
                  pl.BlockSpec((K_SLICES, MXU_DIM, STATE_PER_SLICE), fixed3),
                  pl.BlockSpec((1, N_STATE), fixed2),
                  pl.BlockSpec((1, N_STATE), fixed2),
                  pl.BlockSpec((K_SLICES, STATE_PER_SLICE, MXU_DIM), fixed3),
                  pl.BlockSpec((K_SLICES, STATE_PER_SLICE, MXU_DIM), fixed3),
                  pl.BlockSpec((1, D_SSM), fixed2)],
        out_specs=pl.BlockSpec((batch, SCAN_T, D_SSM), chunk),
        out_shape=jax.ShapeDtypeStruct((batch, seq, D_SSM), F32),
        scratch_shapes=[pltpu.VMEM((SSM_SLABS, rows, LANES), F32), pltpu.VMEM((SSM_SLABS, rows, LANES), F32),
                        pltpu.VMEM((rows, N_STATE), F32), pltpu.VMEM((rows, N_STATE), F32),
                        pltpu.VMEM((SUBLANES, N_STATE), F32), pltpu.VMEM((SUBLANES, N_STATE), F32)],
        compiler_params=pltpu.CompilerParams(dimension_semantics=("arbitrary",),
                                             vmem_limit_bytes=VMEM_LIMIT_BYTES),
        name="ssm_scan",
    )(u3, bre, bim, lbre, lbim, cre, cim, d)


def _post_kernel(x_ref, oa_ref, y_ref, wglu_ref, gs_ref, wout_ref, g2_ref,
                 wg_ref, wu_ref, wd_ref, gf_ref, o_ref):
    yg = jax.nn.gelu(y_ref[...])
    gate = jnp.dot(yg.astype(BF16), wglu_ref[...], preferred_element_type=F32)
    y2 = yg * jax.nn.sigmoid(gate)
    o_ssm = (_rms(y2) * gs_ref[...]).astype(BF16)
    x1 = x_ref[...] + jnp.dot(oa_ref[...], wout_ref[0:D_ATTN, :], preferred_element_type=F32)
    x1 = x1 + jnp.dot(o_ssm, wout_ref[D_ATTN:, :], preferred_element_type=F32)
    h2 = (_rms(x1) * g2_ref[...]).astype(BF16)
    o_ref[...] = x1
    for c in range(D_FF // FF_CHUNK):
        cols = slice(c * FF_CHUNK, (c + 1) * FF_CHUNK)
        gt = jnp.dot(h2, wg_ref[:, cols], preferred_element_type=F32)
        up = jnp.dot(h2, wu_ref[:, cols], preferred_element_type=F32)
        a = (gt * jax.nn.sigmoid(gt) * up).astype(BF16)
        o_ref[...] += jnp.dot(a, wd_ref[cols, :], preferred_element_type=F32)
    o_ref[...] = _rms(o_ref[...]) * gf_ref[...]


def _post(x2, oa, y, wglu, gs, wout, g2, wg, wu, wd, gf):
    n = x2.shape[0]
    row = lambda i: (i, 0)
    fixed = lambda i: (0, 0)
    once = functools.partial(pl.BlockSpec, index_map=fixed, pipeline_mode=pl.Buffered(1))
    return pl.pallas_call(
        _post_kernel,
        grid=(n // TM_POST,),
        in_specs=[pl.BlockSpec((TM_POST, D_MODEL), row),
                  pl.BlockSpec((TM_POST, D_ATTN), row),
                  pl.BlockSpec((TM_POST, D_SSM), row),
                  once((D_SSM, D_SSM)),
                  once((1, D_SSM)),
                  once((D_MODEL, D_MODEL)),
                  once((1, D_MODEL)),
                  once((D_MODEL, D_FF)),
                  once((D_MODEL, D_FF)),
                  once((D_FF, D_MODEL)),
                  once((1, D_MODEL))],
        out_specs=pl.BlockSpec((TM_POST, D_MODEL), row),
        out_shape=jax.ShapeDtypeStruct((n, D_MODEL), F32),
        compiler_params=pltpu.CompilerParams(dimension_semantics=("arbitrary",),
                                             vmem_limit_bytes=VMEM_LIMIT_BYTES),
        name="post",
    )(x2, oa, y, wglu, gs, wout, g2, wg, wu, wd, gf)


def _block_diag_b(bb):
    gl = MXU_DIM // SSM_GROUP
    eye = jnp.eye(gl, dtype=bb.dtype)
    t = bb.reshape(K_SLICES, gl, SSM_GROUP, SSM_STATE)
    m = t[:, :, :, None, :] * eye[None, :, None, :, None]
    return m.reshape(K_SLICES, MXU_DIM, STATE_PER_SLICE).astype(BF16)


def _block_diag_c(c):
    gl = MXU_DIM // SSM_GROUP
    eye = jnp.eye(gl, dtype=c.dtype)
    t = c.reshape(K_SLICES, gl, SSM_GROUP, SSM_STATE).transpose(0, 1, 3, 2)
    m = t[:, :, :, None, :] * eye[None, :, None, :, None]
    return m.reshape(K_SLICES, STATE_PER_SLICE, MXU_DIM).astype(BF16)


def _cumsum_matrix():
    j = jnp.arange(BLK)[:, None]
    s = jnp.arange(BLK)[None, :]
    tri = -(j >= s).astype(BF16)
    half = jnp.concatenate([tri, -jnp.ones((BLK, BLK), BF16)], axis=1)
    return jnp.concatenate([half, half], axis=0)


def kernel(x, norm1_g, w_in, attn_norm_g, lambda_re, lambda_im, log_step, b_re, b_im, c_re, c_im,
           d_skip, w_glu, ssm_norm_g, w_out, norm2_g, w_gate, w_up, w_down, final_norm_g):
    batch, seq, d_model = x.shape
    assert d_model == D_MODEL and norm1_g.shape[0] == 1 and batch == SUBLANES
    n = batch * seq
    x2 = x.reshape(n, D_MODEL)

    q, k, v, u = _in_proj(x2, norm1_g[0][None], w_in[0].astype(BF16))
    oa = _attention(q, k, v, attn_norm_g[0][None], _cumsum_matrix(), batch, seq)

    lb_re, lb_im, bb_re, bb_im = _ssm_prep(
        lambda_re[0], lambda_im[0], log_step[0][:, None],
        b_re[0].transpose(0, 2, 1), b_im[0].transpose(0, 2, 1))
    y = _ssm_scan(u.reshape(batch, seq, D_SSM), _block_diag_b(bb_re), _block_diag_b(bb_im),
                  lb_re.reshape(1, N_STATE), lb_im.reshape(1, N_STATE),
                  _block_diag_c(c_re[0]), _block_diag_c(c_im[0]), d_skip[0].reshape(1, D_SSM))

    out = _post(x2, oa, y.reshape(n, D_SSM), w_glu[0].astype(BF16), ssm_norm_g[0][None],
                w_out[0].astype(BF16), norm2_g[0][None], w_gate[0].astype(BF16),
                w_up[0].astype(BF16), w_down[0].astype(BF16), final_norm_g[None])
    return out.reshape(batch, seq, D_MODEL)
```

```python
import functools

import jax
import jax.numpy as jnp
from jax import lax
from jax.experimental import pallas as pl
from jax.experimental.pallas import tpu as pltpu

F32 = jnp.float32
BF16 = jnp.bfloat16

D_MODEL = 1024
ATTN_HEADS = 8
HEAD_DIM = 64
D_ATTN = ATTN_HEADS * HEAD_DIM
D_SSM = D_MODEL - D_ATTN
SSM_GROUP = 16
SSM_GROUPS = D_SSM // SSM_GROUP
SSM_STATE = 64
N_STATE = SSM_GROUPS * SSM_STATE
D_FF = 2816
D_IN = 3 * D_ATTN + D_SSM
EPS = 1e-6

LANES = 128
SUBLANES = 8
MXU_DIM = 256
VMEM_LIMIT_BYTES = 56 * 1024 * 1024

TM_PROJ = 512
TM_POST = 512
FF_CHUNK = 256
BLK = 128
HEAD_PAIRS = ATTN_HEADS // 2
Q_CHAINS = 4
SCAN_T = 64
SCAN_COLS = 512
SCAN_CHUNKS = N_STATE // SCAN_COLS
K_SLICES = D_SSM // MXU_DIM
STATE_PER_SLICE = N_STATE // K_SLICES
SSM_SLABS = D_SSM // LANES
LOG_W_ZERO = -104.0
MASKED = 1e30
LOG2_E = 1.4426950408889634


def _rms(x):
    return x * lax.rsqrt(jnp.mean(x * x, axis=-1, keepdims=True) + EPS)


def _ssm_prep_kernel(lre_ref, lim_ref, ls_ref, bre_ref, bim_ref,
                     lbre_ref, lbim_ref, bbre_ref, bbim_ref):
    lam_re = lre_ref[...]
    lam_im = lim_ref[...]
    dt = jnp.exp(ls_ref[...])
    mag = jnp.exp(lam_re * dt)
    ang = lam_im * dt
    lb_re = mag * jnp.cos(ang)
    lb_im = mag * jnp.sin(ang)
    den = lam_re * lam_re + lam_im * lam_im
    num_re = lb_re - 1.0
    f_re = (num_re * lam_re + lb_im * lam_im) / den
    f_im = (lb_im * lam_re - num_re * lam_im) / den
    lbre_ref[...] = lb_re
    lbim_ref[...] = lb_im
    br = bre_ref[...]
    bi = bim_ref[...]
    bbre_ref[...] = f_re[:, None, :] * br - f_im[:, None, :] * bi
    bbim_ref[...] = f_re[:, None, :] * bi + f_im[:, None, :] * br


def _ssm_prep(lambda_re, lambda_im, log_step, b_re_t, b_im_t):
    g, p = lambda_re.shape
    h = b_re_t.shape[1]
    return pl.pallas_call(
        _ssm_prep_kernel,
        out_shape=(jax.ShapeDtypeStruct((g, p), F32), jax.ShapeDtypeStruct((g, p), F32),
                   jax.ShapeDtypeStruct((g, h, p), F32), jax.ShapeDtypeStruct((g, h, p), F32)),
        name="ssm_prep",
    )(lambda_re, lambda_im, log_step, b_re_t, b_im_t)


def _in_proj_kernel(x_ref, g_ref, w_ref, q_ref, k_ref, v_ref, u_ref):
    h = (_rms(x_ref[...]) * g_ref[...]).astype(BF16)
    scale = HEAD_DIM ** -0.5
    q = jnp.dot(h, w_ref[:, 0:D_ATTN], preferred_element_type=F32)
    q_ref[...] = (q * scale).astype(BF16)
    k = jnp.dot(h, w_ref[:, D_ATTN:2 * D_ATTN], preferred_element_type=F32)
    k_ref[...] = k.astype(BF16)
    v = jnp.dot(h, w_ref[:, 2 * D_ATTN:3 * D_ATTN], preferred_element_type=F32)
    v_ref[...] = v.astype(BF16)
    u_ref[...] = jnp.dot(h, w_ref[:, 3 * D_ATTN:], preferred_element_type=F32)


def _in_proj(x2, g, w):
    n = x2.shape[0]
    row = lambda i: (i, 0)
    fixed = lambda i: (0, 0)
    return pl.pallas_call(
        _in_proj_kernel,
        grid=(n // TM_PROJ,),
        in_specs=[pl.BlockSpec((TM_PROJ, D_MODEL), row),
                  pl.BlockSpec((1, D_MODEL), fixed),
                  pl.BlockSpec((D_MODEL, D_IN), fixed)],
        out_specs=[pl.BlockSpec((TM_PROJ, D_ATTN), row)] * 3 + [pl.BlockSpec((TM_PROJ, D_SSM), row)],
        out_shape=[jax.ShapeDtypeStruct((n, D_ATTN), BF16)] * 3 + [jax.ShapeDtypeStruct((n, D_SSM), F32)],
        compiler_params=pltpu.CompilerParams(dimension_semantics=("arbitrary",),
                                             vmem_limit_bytes=VMEM_LIMIT_BYTES),
        name="in_proj",
    )(x2, g, w)


def _attn_kernel(q_ref, k_ref, v_ref, g_ref, cmat_ref, o_ref, qm_scr, z_scr, w_scr, r_scr, acc_scr):
    step = pl.program_id(1)
    row = lax.broadcasted_iota(jnp.int32, (BLK, BLK), 0)
    col = lax.broadcasted_iota(jnp.int32, (BLK, BLK), 1)
    before = col < row
    low_half = col < HEAD_DIM
    high_half = jnp.logical_not(low_half)
    cmat = cmat_ref[...]
    nt_dims = (((1,), (1,)), ((), ()))
    first = [step * Q_CHAINS + c for c in range(Q_CHAINS)]

    def key_rows(j, c):
        return pl.ds(pl.multiple_of(jnp.clip(j, 0, first[c]) * BLK, BLK), BLK)

    def scores(j, c, p):
        kp = k_ref[key_rows(j, c), p * LANES:(p + 1) * LANES]
        return lax.dot_general(qm_scr[c * HEAD_PAIRS + p], kp, nt_dims,
                               preferred_element_type=F32)

    def apply_weights(j, c, p):
        valid = j >= 0
        vp = v_ref[key_rows(j, c), p * LANES:(p + 1) * LANES]
        vzero = jnp.zeros_like(vp)
        v2 = jnp.concatenate([jnp.where(jnp.logical_and(low_half, valid), vp, vzero),
                              jnp.where(jnp.logical_and(high_half, valid), vp, vzero)], axis=0)
        u = c * HEAD_PAIRS + p
        acc_scr[u] += jnp.dot(w_scr[u], v2, preferred_element_type=F32)

    for c in range(Q_CHAINS):
        for p in range(HEAD_PAIRS):
            u = c * HEAD_PAIRS + p
            qp = q_ref[c * BLK:(c + 1) * BLK, p * LANES:(p + 1) * LANES]
            zero = jnp.zeros_like(qp)
            qm_scr[u, 0:BLK] = jnp.where(low_half, qp, zero)
            qm_scr[u, BLK:2 * BLK] = jnp.where(low_half, zero, qp)
            acc_scr[u] = jnp.zeros((BLK, LANES), F32)
            w_scr[u] = jnp.zeros((BLK, 2 * BLK), BF16)
        for h in range(ATTN_HEADS):
            r_scr[c * ATTN_HEADS + h] = jnp.zeros((BLK, BLK), F32)
    for c in range(Q_CHAINS):
        for p in range(HEAD_PAIRS):
            z = scores(first[c], c, p)
            for hh in range(2):
                rows = slice(hh * BLK, (hh + 1) * BLK)
                z_scr[c * HEAD_PAIRS + p, rows] = jnp.where(before, z[rows], -MASKED)

    units = [(c, h) for c in range(Q_CHAINS) for h in range(ATTN_HEADS)]

    def body(carry):
        t, _ = carry
        for c in range(Q_CHAINS):
            for p in range(HEAD_PAIRS):
                apply_weights(first[c] - t + 1, c, p)
        splits = []
        for c, h in units:
            z = z_scr[c * HEAD_PAIRS + h // 2, (h % 2) * BLK:(h % 2 + 1) * BLK]
            sp = jnp.maximum(z, 0.0) + jnp.log(1.0 + jnp.exp2(jnp.abs(z) * -LOG2_E))
            hi = sp.astype(BF16)
            lo = (sp - hi.astype(F32)).astype(BF16)
            splits.append(jnp.concatenate([hi, lo], axis=1))
        cs = jnp.dot(jnp.concatenate(splits, axis=0), cmat,
                     preferred_element_type=F32)
        worst = [None] * Q_CHAINS
        for n, (c, h) in enumerate(units):
            csn = cs[n * BLK:(n + 1) * BLK]
            r = r_scr[n]
            r_new = r + csn[:, BLK:]
            r_scr[n] = r_new
            worst[c] = r_new if worst[c] is None else jnp.maximum(worst[c], r_new)
            z = z_scr[c * HEAD_PAIRS + h // 2, (h % 2) * BLK:(h % 2 + 1) * BLK]
            w = jnp.exp2((z + csn[:, :BLK] + r) * LOG2_E)
            w_scr[c * HEAD_PAIRS + h // 2, :, (h % 2) * BLK:(h % 2 + 1) * BLK] = w.astype(BF16)
        for c in range(Q_CHAINS):
            for p in range(HEAD_PAIRS):
                z_scr[c * HEAD_PAIRS + p] = scores(first[c] - t - 1, c, p)
        m = None
        for c in range(Q_CHAINS):
            mc = jnp.where(first[c] - t - 1 >= 0, jnp.max(worst[c]), -jnp.inf)
            m = mc if m is None else jnp.maximum(m, mc)
        return t + 1, m

    def cond(carry):
        return carry[1] > LOG_W_ZERO

    t_end, _ = lax.while_loop(cond, body, (jnp.int32(0), jnp.float32(0.0)))
    for c in range(Q_CHAINS):
        for p in range(HEAD_PAIRS):
            apply_weights(first[c] - t_end + 1, c, p)

    for c in range(Q_CHAINS):
        o = jnp.concatenate([acc_scr[c * HEAD_PAIRS + p] for p in range(HEAD_PAIRS)], axis=1)
        o_ref[c * BLK:(c + 1) * BLK, :] = (_rms(o) * g_ref[...]).astype(BF16)


def _attention(q, k, v, g, cmat, batch, seq):
    n = q.shape[0]
    rows = Q_CHAINS * BLK
    steps = seq // rows
    qrow = lambda b, i: (b * steps + i, 0)
    kv = lambda b, i: (b, 0)
    fixed = lambda b, i: (0, 0)
    pairs = Q_CHAINS * HEAD_PAIRS
    return pl.pallas_call(
        _attn_kernel,
        grid=(batch, steps),
        in_specs=[pl.BlockSpec((rows, D_ATTN), qrow),
                  pl.BlockSpec((seq, D_ATTN), kv),
                  pl.BlockSpec((seq, D_ATTN), kv),
                  pl.BlockSpec((1, D_ATTN), fixed),
                  pl.BlockSpec((2 * BLK, 2 * BLK), fixed)],
        out_specs=pl.BlockSpec((rows, D_ATTN), qrow),
        out_shape=jax.ShapeDtypeStruct((n, D_ATTN), BF16),
        scratch_shapes=[pltpu.VMEM((pairs, 2 * BLK, LANES), BF16),
                        pltpu.VMEM((pairs, 2 * BLK, BLK), F32),
                        pltpu.VMEM((pairs, BLK, 2 * BLK), BF16),
                        pltpu.VMEM((Q_CHAINS * ATTN_HEADS, BLK, BLK), F32),
                        pltpu.VMEM((pairs, BLK, LANES), F32)],
        compiler_params=pltpu.CompilerParams(dimension_semantics=("arbitrary", "arbitrary"),
                                             vmem_limit_bytes=VMEM_LIMIT_BYTES),
        name="attention",
    )(q, k, v, g, cmat)


def _ssm_kernel(u_ref, bre_ref, bim_ref, lbre_ref, lbim_ref, cre_ref, cim_ref, d_ref,
                y_ref, utb_scr, ytb_scr, sr_scr, si_scr, *x_scr):
    xr_scr, xi_scr = x_scr[:SCAN_CHUNKS], x_scr[SCAN_CHUNKS:]

    @pl.when(pl.program_id(0) == 0)
    def _():
        sr_scr[...] = jnp.zeros_like(sr_scr)
        si_scr[...] = jnp.zeros_like(si_scr)

    for b in range(SUBLANES):
        for s in range(SSM_SLABS):
            utb_scr[s, pl.ds(b, SCAN_T, stride=SUBLANES), :] = u_ref[b, :, s * LANES:(s + 1) * LANES]

    slabs_per_slice = MXU_DIM // LANES
    chunks_per_slice = SCAN_CHUNKS // K_SLICES
    uk = [jnp.concatenate([utb_scr[k * slabs_per_slice + s] for s in range(slabs_per_slice)],
                          axis=1).astype(BF16) for k in range(K_SLICES)]

    def project_in(c):
        k, part = divmod(c, chunks_per_slice)
        cols = slice(part * SCAN_COLS, (part + 1) * SCAN_COLS)
        xr_scr[c][...] = jnp.dot(uk[k], bre_ref[k, :, cols], preferred_element_type=F32)
        xi_scr[c][...] = jnp.dot(uk[k], bim_ref[k, :, cols], preferred_element_type=F32)

    def recur(c):
        cols = slice(c * SCAN_COLS, (c + 1) * SCAN_COLS)
        lr = jnp.broadcast_to(lbre_ref[:, cols], (SUBLANES, SCAN_COLS))
        li = jnp.broadcast_to(lbim_ref[:, cols], (SUBLANES, SCAN_COLS))
        sr, si = sr_scr[:, cols], si_scr[:, cols]
        for t in range(SCAN_T):
            rows = slice(t * SUBLANES, (t + 1) * SUBLANES)
            sr, si = (lr * sr - li * si + xr_scr[c][rows, :],
                      lr * si + li * sr + xi_scr[c][rows, :])
            xr_scr[c][rows, :] = sr
            xi_scr[c][rows, :] = si
        sr_scr[:, cols] = sr
        si_scr[:, cols] = si

    def project_out(c):
        k, part = divmod(c, chunks_per_slice)
        rows = slice(part * SCAN_COLS, (part + 1) * SCAN_COLS)
        y = jnp.dot(xr_scr[c][...].astype(BF16), cre_ref[k, rows, :], preferred_element_type=F32)
        y = y - jnp.dot(xi_scr[c][...].astype(BF16), cim_ref[k, rows, :], preferred_element_type=F32)
        for s in range(slabs_per_slice):
            slab = k * slabs_per_slice + s
            lanes = slice(slab * LANES, (slab + 1) * LANES)
            ys = y[:, s * LANES:(s + 1) * LANES]
            if part == 0:
                ytb_scr[slab] = ys + d_ref[:, lanes] * utb_scr[slab]
            else:
                ytb_scr[slab] += ys

    for stage in range(SCAN_CHUNKS + 2):
        if stage < SCAN_CHUNKS:
            project_in(stage)
        if 1 <= stage <= SCAN_CHUNKS:
            recur(stage - 1)
        if stage >= 2:
            project_out(stage - 2)

    for b in range(SUBLANES):
        for s in range(SSM_SLABS):
            y_ref[b, :, s * LANES:(s + 1) * LANES] = ytb_scr[s, pl.ds(b, SCAN_T, stride=SUBLANES), :]


def _ssm_scan(u3, bre, bim, lbre, lbim, cre, cim, d):
    batch, seq, _ = u3.shape
    rows = SCAN_T * batch
    chunk = lambda i: (0, i, 0)
    fixed2 = lambda i: (0, 0)
    fixed3 = lambda i: (0, 0, 0)
    return pl.pallas_call(
        _ssm_kernel,
        grid=(seq // SCAN_T,),
        in_specs=[pl.BlockSpec((batch, SCAN_T, D_SSM), chunk),
                  pl.BlockSpec((K_SLICES, MXU_DIM, STATE_PER_SLICE), fixed3),
                  pl.BlockSpec((K_SLICES, MXU_DIM, STATE_PER_SLICE), fixed3),
                  pl.BlockSpec((1, N_STATE), fixed2),
                  pl.BlockSpec((1, N_STATE), fixed2),
                  pl.BlockSpec((K_SLICES, STATE_PER_SLICE, MXU_DIM), fixed3),
                  pl.BlockSpec((K_SLICES, STATE_PER_SLICE, MXU_DIM), fixed3),
                  pl.BlockSpec((1, D_SSM), fixed2)],
        out_specs=pl.BlockSpec((batch, SCAN_T, D_SSM), chunk),
        out_shape=jax.ShapeDtypeStruct((batch, seq, D_SSM), F32),
        scratch_shapes=[pltpu.VMEM((SSM_SLABS, rows, LANES), F32), pltpu.VMEM((SSM_SLABS, rows, LANES), F32),
                        pltpu.VMEM((SUBLANES, N_STATE), F32), pltpu.VMEM((SUBLANES, N_STATE), F32)]
        + [pltpu.VMEM((rows, SCAN_COLS), F32)] * (2 * SCAN_CHUNKS),
        compiler_params=pltpu.CompilerParams(dimension_semantics=("arbitrary",),
                                             vmem_limit_bytes=VMEM_LIMIT_BYTES),
        name="ssm_scan",
    )(u3, bre, bim, lbre, lbim, cre, cim, d)


def _post_kernel(x_ref, oa_ref, y_ref, wglu_ref, gs_ref, wout_ref, g2_ref,
                 wg_ref, wu_ref, wd_ref, gf_ref, o_ref):
    yg = jax.nn.gelu(y_ref[...])
    gate = jnp.dot(yg.astype(BF16), wglu_ref[...], preferred_element_type=F32)
    y2 = yg * jax.nn.sigmoid(gate)
    o_ssm = (_rms(y2) * gs_ref[...]).astype(BF16)
    x1 = x_ref[...] + jnp.dot(oa_ref[...], wout_ref[0:D_ATTN, :], preferred_element_type=F32)
    x1 = x1 + jnp.dot(o_ssm, wout_ref[D_ATTN:, :], preferred_element_type=F32)
    h2 = (_rms(x1) * g2_ref[...]).astype(BF16)
    o_ref[...] = x1
    for c in range(D_FF // FF_CHUNK):
        cols = slice(c * FF_CHUNK, (c + 1) * FF_CHUNK)
        gt = jnp.dot(h2, wg_ref[:, cols], preferred_element_type=F32)
        up = jnp.dot(h2, wu_ref[:, cols], preferred_element_type=F32)
        a = (gt * jax.nn.sigmoid(gt) * up).astype(BF16)
        o_ref[...] += jnp.dot(a, wd_ref[cols, :], preferred_element_type=F32)
    o_ref[...] = _rms(o_ref[...]) * gf_ref[...]


def _post(x2, oa, y, wglu, gs, wout, g2, wg, wu, wd, gf):
    n = x2.shape[0]
    row = lambda i: (i, 0)
    fixed = lambda i: (0, 0)
    once = functools.partial(pl.BlockSpec, index_map=fixed, pipeline_mode=pl.Buffered(1))
    return pl.pallas_call(
        _post_kernel,
        grid=(n // TM_POST,),
        in_specs=[pl.BlockSpec((TM_POST, D_MODEL), row),
                  pl.BlockSpec((TM_POST, D_ATTN), row),
                  pl.BlockSpec((TM_POST, D_SSM), row),
                  once((D_SSM, D_SSM)),
                  once((1, D_SSM)),
                  once((D_MODEL, D_MODEL)),
                  once((1, D_MODEL)),
                  once((D_MODEL, D_FF)),
                  once((D_MODEL, D_FF)),
                  once((D_FF, D_MODEL)),
                  once((1, D_MODEL))],
        out_specs=pl.BlockSpec((TM_POST, D_MODEL), row),
        out_shape=jax.ShapeDtypeStruct((n, D_MODEL), F32),
        compiler_params=pltpu.CompilerParams(dimension_semantics=("arbitrary",),
                                             vmem_limit_bytes=VMEM_LIMIT_BYTES),
        name="post",
    )(x2, oa, y, wglu, gs, wout, g2, wg, wu, wd, gf)


def _block_diag_b(bb):
    gl = MXU_DIM // SSM_GROUP
    eye = jnp.eye(gl, dtype=bb.dtype)
    t = bb.reshape(K_SLICES, gl, SSM_GROUP, SSM_STATE)
    m = t[:, :, :, None, :] * eye[None, :, None, :, None]
    return m.reshape(K_SLICES, MXU_DIM, STATE_PER_SLICE).astype(BF16)


def _block_diag_c(c):
    gl = MXU_DIM // SSM_GROUP
    eye = jnp.eye(gl, dtype=c.dtype)
    t = c.reshape(K_SLICES, gl, SSM_GROUP, SSM_STATE).transpose(0, 1, 3, 2)
    m = t[:, :, :, None, :] * eye[None, :, None, :, None]
    return m.reshape(K_SLICES, STATE_PER_SLICE, MXU_DIM).astype(BF16)


def _cumsum_matrix():
    j = jnp.arange(BLK)[:, None]
    s = jnp.arange(BLK)[None, :]
    tri = -(j >= s).astype(BF16)
    half = jnp.concatenate([tri, -jnp.ones((BLK, BLK), BF16)], axis=1)
    return jnp.concatenate([half, half], axis=0)


def kernel(x, norm1_g, w_in, attn_norm_g, lambda_re, lambda_im, log_step, b_re, b_im, c_re, c_im,
           d_skip, w_glu, ssm_norm_g, w_out, norm2_g, w_gate, w_up, w_down, final_norm_g):
    batch, seq, d_model = x.shape
    assert d_model == D_MODEL and norm1_g.shape[0] == 1 and batch == SUBLANES
    n = batch * seq
    x2 = x.reshape(n, D_MODEL)

    q, k, v, u = _in_proj(x2, norm1_g[0][None], w_in[0].astype(BF16))
    oa = _attention(q, k, v, attn_norm_g[0][None], _cumsum_matrix(), batch, seq)

    lb_re, lb_im, bb_re, bb_im = _ssm_prep(
        lambda_re[0], lambda_im[0], log_step[0][:, None],
        b_re[0].transpose(0, 2, 1), b_im[0].transpose(0, 2, 1))
    y = _ssm_scan(u.reshape(batch, seq, D_SSM), _block_diag_b(bb_re), _block_diag_b(bb_im),
                  lb_re.reshape(1, N_STATE), lb_im.reshape(1, N_STATE),
                  _block_diag_c(c_re[0]), _block_diag_c(c_im[0]), d_skip[0].reshape(1, D_SSM))

    out = _post(x2, oa, y.reshape(n, D_SSM), w_glu[0].astype(BF16), ssm_norm_g[0][None],
                w_out[0].astype(BF16), norm2_g[0][None], w_gate[0].astype(BF16),
                w_up[0].astype(BF16), w_down[0].astype(BF16), final_norm_g[None])
    return out.reshape(batch, seq, D_MODEL)
```

```python
import functools

import jax
import jax.numpy as jnp
from jax import lax
from jax.experimental import pallas as pl
from jax.experimental.pallas import tpu as pltpu

F32 = jnp.float32
BF16 = jnp.bfloat16

D_MODEL = 1024
ATTN_HEADS = 8
HEAD_DIM = 64
D_ATTN = ATTN_HEADS * HEAD_DIM
D_SSM = D_MODEL - D_ATTN
SSM_GROUP = 16
SSM_GROUPS = D_SSM // SSM_GROUP
SSM_STATE = 64
N_STATE = SSM_GROUPS * SSM_STATE
D_FF = 2816
D_IN = 3 * D_ATTN + D_SSM
EPS = 1e-6

LANES = 128
SUBLANES = 8
MXU_DIM = 256
VMEM_LIMIT_BYTES = 56 * 1024 * 1024

SUB_ROWS = 512
TM_PROJ = 2 * SUB_ROWS
TM_POST = 2 * SUB_ROWS
FF_CHUNK = 256
BLK = 128
HEAD_PAIRS = ATTN_HEADS // 2
Q_CHAINS = 4
SCAN_T = 64
SCAN_COLS = 512
SCAN_CHUNKS = N_STATE // SCAN_COLS
K_SLICES = D_SSM // MXU_DIM
STATE_PER_SLICE = N_STATE // K_SLICES
SSM_SLABS = D_SSM // LANES
LOG_W_ZERO = -104.0
MASKED = 1e30
LOG2_E = 1.4426950408889634


def _rms(x):
    return x * lax.rsqrt(jnp.mean(x * x, axis=-1, keepdims=True) + EPS)


def _ssm_prep_kernel(lre_ref, lim_ref, ls_ref, bre_ref, bim_ref,
                     lbre_ref, lbim_ref, bbre_ref, bbim_ref):
    lam_re = lre_ref[...]
    lam_im = lim_ref[...]
    dt = jnp.exp(ls_ref[...])
    mag = jnp.exp(lam_re * dt)
    ang = lam_im * dt
    lb_re = mag * jnp.cos(ang)
    lb_im = mag * jnp.sin(ang)
    den = lam_re * lam_re + lam_im * lam_im
    num_re = lb_re - 1.0
    f_re = (num_re * lam_re + lb_im * lam_im) / den
    f_im = (lb_im * lam_re - num_re * lam_im) / den
    lbre_ref[...] = lb_re
    lbim_ref[...] = lb_im
    br = bre_ref[...]
    bi = bim_ref[...]
    bbre_ref[...] = f_re[:, None, :] * br - f_im[:, None, :] * bi
    bbim_ref[...] = f_re[:, None, :] * bi + f_im[:, None, :] * br


def _ssm_prep(lambda_re, lambda_im, log_step, b_re_t, b_im_t):
    g, p = lambda_re.shape
    h = b_re_t.shape[1]
    return pl.pallas_call(
        _ssm_prep_kernel,
        out_shape=(jax.ShapeDtypeStruct((g, p), F32), jax.ShapeDtypeStruct((g, p), F32),
                   jax.ShapeDtypeStruct((g, h, p), F32), jax.ShapeDtypeStruct((g, h, p), F32)),
        name="ssm_prep",
    )(lambda_re, lambda_im, log_step, b_re_t, b_im_t)


def _in_proj_kernel(x_ref, g_ref, w_ref, q_ref, k_ref, v_ref, u_ref):
    scale = HEAD_DIM ** -0.5
    low_half = lax.broadcasted_iota(jnp.int32, (SUB_ROWS, LANES), 1) < HEAD_DIM
    for s in range(TM_PROJ // SUB_ROWS):
        rows = slice(s * SUB_ROWS, (s + 1) * SUB_ROWS)
        h = (_rms(x_ref[rows, :]) * g_ref[...]).astype(BF16)
        q = jnp.dot(h, w_ref[:, 0:D_ATTN], preferred_element_type=F32)
        q_ref[rows, :] = (q * scale).astype(BF16)
        k = jnp.dot(h, w_ref[:, D_ATTN:2 * D_ATTN], preferred_element_type=F32)
        v = jnp.dot(h, w_ref[:, 2 * D_ATTN:3 * D_ATTN], preferred_element_type=F32)
        for full, ref in ((k, k_ref), (v, v_ref)):
            for p in range(HEAD_PAIRS):
                pair = full[:, p * LANES:(p + 1) * LANES].astype(BF16)
                zero = jnp.zeros_like(pair)
                ref[rows, (2 * p) * LANES:(2 * p + 1) * LANES] = jnp.where(low_half, pair, zero)
                ref[rows, (2 * p + 1) * LANES:(2 * p + 2) * LANES] = jnp.where(low_half, zero, pair)
        u_ref[rows, :] = jnp.dot(h, w_ref[:, 3 * D_ATTN:], preferred_element_type=F32)


def _in_proj(x2, g, w):
    n = x2.shape[0]
    row = lambda i: (i, 0)
    fixed = lambda i: (0, 0)
    return pl.pallas_call(
        _in_proj_kernel,
        grid=(n // TM_PROJ,),
        in_specs=[pl.BlockSpec((TM_PROJ, D_MODEL), row),
                  pl.BlockSpec((1, D_MODEL), fixed),
                  pl.BlockSpec((D_MODEL, D_IN), fixed)],
        out_specs=[pl.BlockSpec((TM_PROJ, D_ATTN), row), pl.BlockSpec((TM_PROJ, 2 * D_ATTN), row),
                   pl.BlockSpec((TM_PROJ, 2 * D_ATTN), row), pl.BlockSpec((TM_PROJ, D_SSM), row)],
        out_shape=[jax.ShapeDtypeStruct((n, D_ATTN), BF16), jax.ShapeDtypeStruct((n, 2 * D_ATTN), BF16),
                   jax.ShapeDtypeStruct((n, 2 * D_ATTN), BF16), jax.ShapeDtypeStruct((n, D_SSM), F32)],
        compiler_params=pltpu.CompilerParams(dimension_semantics=("arbitrary",),
                                             vmem_limit_bytes=VMEM_LIMIT_BYTES),
        name="in_proj",
    )(x2, g, w)


def _attn_kernel(q_ref, k_ref, v_ref, g_ref, cmat_ref, o_ref, z_scr, w_scr, r_scr, acc_scr):
    step = pl.program_id(1)
    row = lax.broadcasted_iota(jnp.int32, (BLK, BLK), 0)
    col = lax.broadcasted_iota(jnp.int32, (BLK, BLK), 1)
    before = col < row
    cmat = cmat_ref[...]
    nt_dims = (((1,), (1,)), ((), ()))
    first = [step * Q_CHAINS + c for c in range(Q_CHAINS)]

    def key_rows(j, c):
        return pl.ds(pl.multiple_of(jnp.clip(j, 0, first[c]) * BLK, BLK), BLK)

    def per_head_rows(ref, j, c, p):
        rows = key_rows(j, c)
        return jnp.concatenate([ref[rows, (2 * p) * LANES:(2 * p + 1) * LANES],
                                ref[rows, (2 * p + 1) * LANES:(2 * p + 2) * LANES]], axis=0)

    def scores(j, c, p):
        qp = q_ref[c * BLK:(c + 1) * BLK, p * LANES:(p + 1) * LANES]
        return lax.dot_general(qp, per_head_rows(k_ref, j, c, p), nt_dims,
                               preferred_element_type=F32)

    def apply_weights(j, c, p):
        u = c * HEAD_PAIRS + p
        acc_scr[u] += jnp.dot(w_scr[u], per_head_rows(v_ref, j, c, p), preferred_element_type=F32)

    for c in range(Q_CHAINS):
        for p in range(HEAD_PAIRS):
            u = c * HEAD_PAIRS + p
            acc_scr[u] = jnp.zeros((BLK, LANES), F32)
            w_scr[u] = jnp.zeros((BLK, 2 * BLK), BF16)
        for h in range(ATTN_HEADS):
            r_scr[c * ATTN_HEADS + h] = jnp.zeros((BLK, BLK), F32)
    for c in range(Q_CHAINS):
        for p in range(HEAD_PAIRS):
            z = scores(first[c], c, p)
            for hh in range(2):
                keys = slice(hh * BLK, (hh + 1) * BLK)
                z_scr[c * HEAD_PAIRS + p, :, keys] = jnp.where(before, z[:, keys], -MASKED)

    def body(carry):
        t, _ = carry
        m = None
        for c in range(Q_CHAINS):
            splits = []
            for h in range(ATTN_HEADS):
                z = z_scr[c * HEAD_PAIRS + h // 2, :, (h % 2) * BLK:(h % 2 + 1) * BLK]
                sp = jnp.maximum(z, 0.0) + jnp.log(1.0 + jnp.exp2(jnp.abs(z) * -LOG2_E))
                hi = sp.astype(BF16)
                lo = (sp - hi.astype(F32)).astype(BF16)
                splits.append(jnp.concatenate([hi, lo], axis=1))
            cs = jnp.dot(jnp.concatenate(splits, axis=0), cmat,
                         preferred_element_type=F32)
            for p in range(HEAD_PAIRS):
                apply_weights(first[c] - t + 1, c, p)
            worst = None
            for h in range(ATTN_HEADS):
                n = c * ATTN_HEADS + h
                csn = cs[h * BLK:(h + 1) * BLK]
                r = r_scr[n]
                r_new = r + csn[:, BLK:]
                r_scr[n] = r_new
                worst = r_new if worst is None else jnp.maximum(worst, r_new)
                z = z_scr[c * HEAD_PAIRS + h // 2, :, (h % 2) * BLK:(h % 2 + 1) * BLK]
                w = jnp.exp2((z + csn[:, :BLK] + r) * LOG2_E)
                w_scr[c * HEAD_PAIRS + h // 2, :, (h % 2) * BLK:(h % 2 + 1) * BLK] = w.astype(BF16)
            for p in range(HEAD_PAIRS):
                z_scr[c * HEAD_PAIRS + p] = scores(first[c] - t - 1, c, p)
            mc = jnp.where(first[c] - t - 1 >= 0, jnp.max(worst), -jnp.inf)
            m = mc if m is None else jnp.maximum(m, mc)
        for c in range(Q_CHAINS):
            @pl.when(first[c] - t - 1 < 0)
            def _(c=c):
                for h in range(ATTN_HEADS):
                    r_scr[c * ATTN_HEADS + h] = jnp.full((BLK, BLK), -MASKED, F32)
        return t + 1, m

    def cond(carry):
        return carry[1] > LOG_W_ZERO

    t_end, _ = lax.while_loop(cond, body, (jnp.int32(0), jnp.float32(0.0)))
    for c in range(Q_CHAINS):
        for p in range(HEAD_PAIRS):
            apply_weights(first[c] - t_end + 1, c, p)

    for c in range(Q_CHAINS):
        o = jnp.concatenate([acc_scr[c * HEAD_PAIRS + p] for p in range(HEAD_PAIRS)], axis=1)
        o_ref[c * BLK:(c + 1) * BLK, :] = (_rms(o) * g_ref[...]).astype(BF16)


def _attention(q, k, v, g, cmat, batch, seq):
    n = q.shape[0]
    rows = Q_CHAINS * BLK
    steps = seq // rows
    qrow = lambda b, i: (b * steps + i, 0)
    kv = lambda b, i: (b, 0)
    fixed = lambda b, i: (0, 0)
    pairs = Q_CHAINS * HEAD_PAIRS
    return pl.pallas_call(
        _attn_kernel,
        grid=(batch, steps),
        in_specs=[pl.BlockSpec((rows, D_ATTN), qrow),
                  pl.BlockSpec((seq, 2 * D_ATTN), kv),
                  pl.BlockSpec((seq, 2 * D_ATTN), kv),
                  pl.BlockSpec((1, D_ATTN), fixed),
                  pl.BlockSpec((2 * BLK, 2 * BLK), fixed)],
        out_specs=pl.BlockSpec((rows, D_ATTN), qrow),
        out_shape=jax.ShapeDtypeStruct((n, D_ATTN), BF16),
        scratch_shapes=[pltpu.VMEM((pairs, BLK, 2 * BLK), F32),
                        pltpu.VMEM((pairs, BLK, 2 * BLK), BF16),
                        pltpu.VMEM((Q_CHAINS * ATTN_HEADS, BLK, BLK), F32),
                        pltpu.VMEM((pairs, BLK, LANES), F32)],
        compiler_params=pltpu.CompilerParams(dimension_semantics=("arbitrary", "arbitrary"),
                                             vmem_limit_bytes=VMEM_LIMIT_BYTES),
        name="attention",
    )(q, k, v, g, cmat)


def _ssm_kernel(u_ref, bre_ref, bim_ref, lbre_ref, lbim_ref, cre_ref, cim_ref, d_ref,
                y_ref, utb_scr, ytb_scr, sr_scr, si_scr, *x_scr):
    xr_scr, xi_scr = x_scr[:SCAN_CHUNKS], x_scr[SCAN_CHUNKS:]

    @pl.when(pl.program_id(0) == 0)
    def _():
        sr_scr[...] = jnp.zeros_like(sr_scr)
        si_scr[...] = jnp.zeros_like(si_scr)

    for b in range(SUBLANES):
        for s in range(SSM_SLABS):
            utb_scr[s, pl.ds(b, SCAN_T, stride=SUBLANES), :] = u_ref[b, :, s * LANES:(s + 1) * LANES]

    slabs_per_slice = MXU_DIM // LANES
    chunks_per_slice = SCAN_CHUNKS // K_SLICES
    uk = [jnp.concatenate([utb_scr[k * slabs_per_slice + s] for s in range(slabs_per_slice)],
                          axis=1).astype(BF16) for k in range(K_SLICES)]

    def project_in(c):
        k, part = divmod(c, chunks_per_slice)
        cols = slice(part * SCAN_COLS, (part + 1) * SCAN_COLS)
        xr_scr[c][...] = jnp.dot(uk[k], bre_ref[k, :, cols], preferred_element_type=F32)
        xi_scr[c][...] = jnp.dot(uk[k], bim_ref[k, :, cols], preferred_element_type=F32)

    def recur(c):
        cols = slice(c * SCAN_COLS, (c + 1) * SCAN_COLS)
        lr = jnp.broadcast_to(lbre_ref[:, cols], (SUBLANES, SCAN_COLS))
        li = jnp.broadcast_to(lbim_ref[:, cols], (SUBLANES, SCAN_COLS))
        sr, si = sr_scr[:, cols], si_scr[:, cols]
        for t in range(SCAN_T):
            rows = slice(t * SUBLANES, (t + 1) * SUBLANES)
            sr, si = (lr * sr - li * si + xr_scr[c][rows, :],
                      lr * si + li * sr + xi_scr[c][rows, :])
            xr_scr[c][rows, :] = sr
            xi_scr[c][rows, :] = si
        sr_scr[:, cols] = sr
        si_scr[:, cols] = si

    def project_out(c):
        k, part = divmod(c, chunks_per_slice)
        rows = slice(part * SCAN_COLS, (part + 1) * SCAN_COLS)
        y = jnp.dot(xr_scr[c][...].astype(BF16), cre_ref[k, rows, :], preferred_element_type=F32)
        y = y - jnp.dot(xi_scr[c][...].astype(BF16), cim_ref[k, rows, :], preferred_element_type=F32)
        for s in range(slabs_per_slice):
            slab = k * slabs_per_slice + s
            lanes = slice(slab * LANES, (slab + 1) * LANES)
            ys = y[:, s * LANES:(s + 1) * LANES]
            if part == 0:
                ytb_scr[slab] = ys + d_ref[:, lanes] * utb_scr[slab]
            else:
                ytb_scr[slab] += ys

    for stage in range(SCAN_CHUNKS + 2):
        if stage < SCAN_CHUNKS:
            project_in(stage)
        if 1 <= stage <= SCAN_CHUNKS:
            recur(stage - 1)
        if stage >= 2:
            project_out(stage - 2)

    for b in range(SUBLANES):
        for s in range(SSM_SLABS):
            y_ref[b, :, s * LANES:(s + 1) * LANES] = ytb_scr[s, pl.ds(b, SCAN_T, stride=SUBLANES), :]


def _ssm_scan(u3, bre, bim, lbre, lbim, cre, cim, d):
    batch, seq, _ = u3.shape
    rows = SCAN_T * batch
    chunk = lambda i: (0, i, 0)
    fixed2 = lambda i: (0, 0)
    fixed3 = lambda i: (0, 0, 0)
    return pl.pallas_call(
        _ssm_kernel,
        grid=(seq // SCAN_T,),
        in_specs=[pl.BlockSpec((batch, SCAN_T, D_SSM), chunk),
                  pl.BlockSpec((K_SLICES, MXU_DIM, STATE_PER_SLICE), fixed3),
                  pl.BlockSpec((K_SLICES, MXU_DIM, STATE_PER_SLICE), fixed3),
                  pl.BlockSpec((1, N_STATE), fixed2),
                  pl.BlockSpec((1, N_STATE), fixed2),
                  pl.BlockSpec((K_SLICES, STATE_PER_SLICE, MXU_DIM), fixed3),
                  pl.BlockSpec((K_SLICES, STATE_PER_SLICE, MXU_DIM), fixed3),
                  pl.BlockSpec((1, D_SSM), fixed2)],
        out_specs=pl.BlockSpec((batch, SCAN_T, D_SSM), chunk),
        out_shape=jax.ShapeDtypeStruct((batch, seq, D_SSM), F32),
        scratch_shapes=[pltpu.VMEM((SSM_SLABS, rows, LANES), F32), pltpu.VMEM((SSM_SLABS, rows, LANES), F32),
                        pltpu.VMEM((SUBLANES, N_STATE), F32), pltpu.VMEM((SUBLANES, N_STATE), F32)]
        + [pltpu.VMEM((rows, SCAN_COLS), F32)] * (2 * SCAN_CHUNKS),
        compiler_params=pltpu.CompilerParams(dimension_semantics=("arbitrary",),
                                             vmem_limit_bytes=VMEM_LIMIT_BYTES),
        name="ssm_scan",
    )(u3, bre, bim, lbre, lbim, cre, cim, d)


def _post_kernel(x_ref, oa_ref, y_ref, wglu_ref, gs_ref, wout_ref, g2_ref,
                 wg_ref, wu_ref, wd_ref, gf_ref, o_ref):
    for s in range(TM_POST // SUB_ROWS):
        rows = slice(s * SUB_ROWS, (s + 1) * SUB_ROWS)
        yg = jax.nn.gelu(y_ref[rows, :])
        gate = jnp.dot(yg.astype(BF16), wglu_ref[...], preferred_element_type=F32)
        y2 = yg * jax.nn.sigmoid(gate)
        o_ssm = (_rms(y2) * gs_ref[...]).astype(BF16)
        x1 = x_ref[rows, :] + jnp.dot(oa_ref[rows, :], wout_ref[0:D_ATTN, :], preferred_element_type=F32)
        x1 = x1 + jnp.dot(o_ssm, wout_ref[D_ATTN:, :], preferred_element_type=F32)
        h2 = (_rms(x1) * g2_ref[...]).astype(BF16)
        o_ref[rows, :] = x1
        for c in range(D_FF // FF_CHUNK):
            cols = slice(c * FF_CHUNK, (c + 1) * FF_CHUNK)
            gt = jnp.dot(h2, wg_ref[:, cols], preferred_element_type=F32)
            up = jnp.dot(h2, wu_ref[:, cols], preferred_element_type=F32)
            a = (gt * jax.nn.sigmoid(gt) * up).astype(BF16)
            o_ref[rows, :] += jnp.dot(a, wd_ref[cols, :], preferred_element_type=F32)
        o_ref[rows, :] = _rms(o_ref[rows, :]) * gf_ref[...]


def _post(x2, oa, y, wglu, gs, wout, g2, wg, wu, wd, gf):
    n = x2.shape[0]
    row = lambda i: (i, 0)
    fixed = lambda i: (0, 0)
    once = functools.partial(pl.BlockSpec, index_map=fixed, pipeline_mode=pl.Buffered(1))
    return pl.pallas_call(
        _post_kernel,
        grid=(n // TM_POST,),
        in_specs=[pl.BlockSpec((TM_POST, D_MODEL), row),
                  pl.BlockSpec((TM_POST, D_ATTN), row),
                  pl.BlockSpec((TM_POST, D_SSM), row),
                  once((D_SSM, D_SSM)),
                  once((1, D_SSM)),
                  once((D_MODEL, D_MODEL)),
                  once((1, D_MODEL)),
                  once((D_MODEL, D_FF)),
                  once((D_MODEL, D_FF)),
                  once((D_FF, D_MODEL)),
                  once((1, D_MODEL))],
        out_specs=pl.BlockSpec((TM_POST, D_MODEL), row),
        out_shape=jax.ShapeDtypeStruct((n, D_MODEL), F32),
        compiler_params=pltpu.CompilerParams(dimension_semantics=("arbitrary",),
                                             vmem_limit_bytes=VMEM_LIMIT_BYTES),
        name="post",
    )(x2, oa, y, wglu, gs, wout, g2, wg, wu, wd, gf)


def _block_diag_b(bb):
    gl = MXU_DIM // SSM_GROUP
    eye = jnp.eye(gl, dtype=bb.dtype)
    t = bb.reshape(K_SLICES, gl, SSM_GROUP, SSM_STATE)
    m = t[:, :, :, None, :] * eye[None, :, None, :, None]
    return m.reshape(K_SLICES, MXU_DIM, STATE_PER_SLICE).astype(BF16)


def _block_diag_c(c):
    gl = MXU_DIM // SSM_GROUP
    eye = jnp.eye(gl, dtype=c.dtype)
    t = c.reshape(K_SLICES, gl, SSM_GROUP, SSM_STATE).transpose(0, 1, 3, 2)
    m = t[:, :, :, None, :] * eye[None, :, None, :, None]
    return m.reshape(K_SLICES, STATE_PER_SLICE, MXU_DIM).astype(BF16)


def _cumsum_matrix():
    j = jnp.arange(BLK)[:, None]
    s = jnp.arange(BLK)[None, :]
    tri = -(j >= s).astype(BF16)
    half = jnp.concatenate([tri, -jnp.ones((BLK, BLK), BF16)], axis=1)
    return jnp.concatenate([half, half], axis=0)


def kernel(x, norm1_g, w_in, attn_norm_g, lambda_re, lambda_im, log_step, b_re, b_im, c_re, c_im,
           d_skip, w_glu, ssm_norm_g, w_out, norm2_g, w_gate, w_up, w_down, final_norm_g):
    batch, seq, d_model = x.shape
    assert d_model == D_MODEL and norm1_g.shape[0] == 1 and batch == SUBLANES
    n = batch * seq
    x2 = x.reshape(n, D_MODEL)

    q, k, v, u = _in_proj(x2, norm1_g[0][None], w_in[0].astype(BF16))
    oa = _attention(q, k, v, attn_norm_g[0][None], _cumsum_matrix(), batch, seq)

    lb_re, lb_im, bb_re, bb_im = _ssm_prep(
        lambda_re[0], lambda_im[0], log_step[0][:, None],
        b_re[0].transpose(0, 2, 1), b_im[0].transpose(0, 2, 1))
    y = _ssm_scan(u.reshape(batch, seq, D_SSM), _block_diag_b(bb_re), _block_diag_b(bb_im),
                  lb_re.reshape(1, N_STATE), lb_im.reshape(1, N_STATE),
                  _block_diag_c(c_re[0]), _block_diag_c(c_im[0]), d_skip[0].reshape(1, D_SSM))

    out = _post(x2, oa, y.reshape(n, D_SSM), w_glu[0].astype(BF16), ssm_norm_g[0][None],
                w_out[0].astype(BF16), norm2_g[0][None], w_gate[0].astype(BF16),
                w_up[0].astype(BF16), w_down[0].astype(BF16), final_norm_g[None])
    return out.reshape(batch, seq, D_MODEL)
```

```python
import functools

import jax
import jax.numpy as jnp
from jax import lax
from jax.experimental import pallas as pl
from jax.experimental.pallas import tpu as pltpu

F32 = jnp.float32
BF16 = jnp.bfloat16

D_MODEL = 1024
ATTN_HEADS = 8
HEAD_DIM = 64
D_ATTN = ATTN_HEADS * HEAD_DIM
D_SSM = D_MODEL - D_ATTN
SSM_GROUP = 16
SSM_GROUPS = D_SSM // SSM_GROUP
SSM_STATE = 64
N_STATE = SSM_GROUPS * SSM_STATE
D_FF = 2816
D_IN = 3 * D_ATTN + D_SSM
EPS = 1e-6

LANES = 128
SUBLANES = 8
MXU_DIM = 256
VMEM_LIMIT_BYTES = 56 * 1024 * 1024

SUB_ROWS = 512
TM_PROJ = 2 * SUB_ROWS
TM_POST = 2 * SUB_ROWS
FF_CHUNK = 256
BLK = 128
HEAD_PAIRS = ATTN_HEADS // 2
Q_CHAINS = 4
SCAN_T = 64
SCAN_COLS = 512
SCAN_CHUNKS = N_STATE // SCAN_COLS
K_SLICES = D_SSM // MXU_DIM
STATE_PER_SLICE = N_STATE // K_SLICES
SSM_SLABS = D_SSM // LANES
LOG_W_ZERO = -104.0
MASKED = 1e30
LOG2_E = 1.4426950408889634


def _rms(x):
    return x * lax.rsqrt(jnp.mean(x * x, axis=-1, keepdims=True) + EPS)


def _ssm_prep_kernel(lre_ref, lim_ref, ls_ref, bre_ref, bim_ref,
                     lbre_ref, lbim_ref, bbre_ref, bbim_ref):
    lam_re = lre_ref[...]
    lam_im = lim_ref[...]
    dt = jnp.exp(ls_ref[...])
    mag = jnp.exp(lam_re * dt)
    ang = lam_im * dt
    lb_re = mag * jnp.cos(ang)
    lb_im = mag * jnp.sin(ang)
    den = lam_re * lam_re + lam_im * lam_im
    num_re = lb_re - 1.0
    f_re = (num_re * lam_re + lb_im * lam_im) / den
    f_im = (lb_im * lam_re - num_re * lam_im) / den
    lbre_ref[...] = lb_re
    lbim_ref[...] = lb_im
    br = bre_ref[...]
    bi = bim_ref[...]
    bbre_ref[...] = f_re[:, None, :] * br - f_im[:, None, :] * bi
    bbim_ref[...] = f_re[:, None, :] * bi + f_im[:, None, :] * br


def _ssm_prep(lambda_re, lambda_im, log_step, b_re_t, b_im_t):
    g, p = lambda_re.shape
    h = b_re_t.shape[1]
    return pl.pallas_call(
        _ssm_prep_kernel,
        out_shape=(jax.ShapeDtypeStruct((g, p), F32), jax.ShapeDtypeStruct((g, p), F32),
                   jax.ShapeDtypeStruct((g, h, p), F32), jax.ShapeDtypeStruct((g, h, p), F32)),
        name="ssm_prep",
    )(lambda_re, lambda_im, log_step, b_re_t, b_im_t)


def _in_proj_kernel(x_ref, g_ref, w_ref, q_ref, k_ref, v_ref, u_ref):
    scale = HEAD_DIM ** -0.5
    low_half = lax.broadcasted_iota(jnp.int32, (SUB_ROWS, LANES), 1) < HEAD_DIM
    for s in range(TM_PROJ // SUB_ROWS):
        rows = slice(s * SUB_ROWS, (s + 1) * SUB_ROWS)
        h = (_rms(x_ref[rows, :]) * g_ref[...]).astype(BF16)
        q = jnp.dot(h, w_ref[:, 0:D_ATTN], preferred_element_type=F32)
        q_ref[rows, :] = (q * scale).astype(BF16)
        k = jnp.dot(h, w_ref[:, D_ATTN:2 * D_ATTN], preferred_element_type=F32)
        v = jnp.dot(h, w_ref[:, 2 * D_ATTN:3 * D_ATTN], preferred_element_type=F32)
        for full, ref in ((k, k_ref), (v, v_ref)):
            for p in range(HEAD_PAIRS):
                pair = full[:, p * LANES:(p + 1) * LANES].astype(BF16)
                zero = jnp.zeros_like(pair)
                ref[rows, (2 * p) * LANES:(2 * p + 1) * LANES] = jnp.where(low_half, pair, zero)
                ref[rows, (2 * p + 1) * LANES:(2 * p + 2) * LANES] = jnp.where(low_half, zero, pair)
        u_ref[rows, :] = jnp.dot(h, w_ref[:, 3 * D_ATTN:], preferred_element_type=F32)


def _in_proj(x2, g, w):
    n = x2.shape[0]
    row = lambda i: (i, 0)
    fixed = lambda i: (0, 0)
    return pl.pallas_call(
        _in_proj_kernel,
        grid=(n // TM_PROJ,),
        in_specs=[pl.BlockSpec((TM_PROJ, D_MODEL), row),
                  pl.BlockSpec((1, D_MODEL), fixed),
                  pl.BlockSpec((D_MODEL, D_IN), fixed)],
        out_specs=[pl.BlockSpec((TM_PROJ, D_ATTN), row), pl.BlockSpec((TM_PROJ, 2 * D_ATTN), row),
                   pl.BlockSpec((TM_PROJ, 2 * D_ATTN), row), pl.BlockSpec((TM_PROJ, D_SSM), row)],
        out_shape=[jax.ShapeDtypeStruct((n, D_ATTN), BF16), jax.ShapeDtypeStruct((n, 2 * D_ATTN), BF16),
                   jax.ShapeDtypeStruct((n, 2 * D_ATTN), BF16), jax.ShapeDtypeStruct((n, D_SSM), F32)],
        compiler_params=pltpu.CompilerParams(dimension_semantics=("arbitrary",),
                                             vmem_limit_bytes=VMEM_LIMIT_BYTES),
        name="in_proj",
    )(x2, g, w)


def _attn_kernel(q_ref, k_ref, v_ref, g_ref, cmat_ref, o_ref, z_scr, w_scr, r_scr, acc_scr):
    step = pl.program_id(1)
    row = lax.broadcasted_iota(jnp.int32, (BLK, BLK), 0)
    col = lax.broadcasted_iota(jnp.int32, (BLK, BLK), 1)
    before = col < row
    cmat = cmat_ref[...]
    nt_dims = (((1,), (1,)), ((), ()))
    first = [step * Q_CHAINS + c for c in range(Q_CHAINS)]

    def key_rows(j, c):
        return pl.ds(pl.multiple_of(jnp.clip(j, 0, first[c]) * BLK, BLK), BLK)

    def per_head_rows(ref, j, c, p):
        rows = key_rows(j, c)
        return jnp.concatenate([ref[rows, (2 * p) * LANES:(2 * p + 1) * LANES],
                                ref[rows, (2 * p + 1) * LANES:(2 * p + 2) * LANES]], axis=0)

    def scores(j, c, p):
        qp = q_ref[c * BLK:(c + 1) * BLK, p * LANES:(p + 1) * LANES]
        return lax.dot_general(qp, per_head_rows(k_ref, j, c, p), nt_dims,
                               preferred_element_type=F32)

    def apply_weights(j, c, p):
        u = c * HEAD_PAIRS + p
        acc_scr[u] += jnp.dot(w_scr[u], per_head_rows(v_ref, j, c, p), preferred_element_type=F32)

    for c in range(Q_CHAINS):
        for p in range(HEAD_PAIRS):
            z = scores(first[c], c, p)
            for hh in range(2):
                keys = slice(hh * BLK, (hh + 1) * BLK)
                z_scr[c * HEAD_PAIRS + p, :, keys] = jnp.where(before, z[:, keys], -MASKED)
    for u in range(Q_CHAINS * HEAD_PAIRS):
        acc_scr[u] = jnp.zeros((BLK, LANES), F32)
        w_scr[u] = jnp.zeros((BLK, 2 * BLK), BF16)
    for n in range(Q_CHAINS * ATTN_HEADS):
        r_scr[n] = jnp.zeros((BLK, BLK), F32)

    def body(carry):
        t, _ = carry
        m = None
        for c in range(Q_CHAINS):
            sps = []
            for h in range(ATTN_HEADS):
                z = z_scr[c * HEAD_PAIRS + h // 2, :, (h % 2) * BLK:(h % 2 + 1) * BLK]
                sp = jnp.maximum(z, 0.0) + jnp.log(1.0 + jnp.exp2(jnp.abs(z) * -LOG2_E))
                sps.append(sp.astype(BF16))
            cs = jnp.dot(jnp.concatenate(sps, axis=0), cmat,
                         preferred_element_type=F32)
            for p in range(HEAD_PAIRS):
                apply_weights(first[c] - t + 1, c, p)
            worst = None
            for h in range(ATTN_HEADS):
                n = c * ATTN_HEADS + h
                csn = cs[h * BLK:(h + 1) * BLK]
                r = r_scr[n]
                r_new = r + csn[:, BLK:]
                r_scr[n] = r_new
                worst = r_new if worst is None else jnp.maximum(worst, r_new)
                z = z_scr[c * HEAD_PAIRS + h // 2, :, (h % 2) * BLK:(h % 2 + 1) * BLK]
                w = jnp.exp2((z + csn[:, :BLK] + r) * LOG2_E)
                w_scr[c * HEAD_PAIRS + h // 2, :, (h % 2) * BLK:(h % 2 + 1) * BLK] = w.astype(BF16)
            for p in range(HEAD_PAIRS):
                z_scr[c * HEAD_PAIRS + p] = scores(first[c] - t - 1, c, p)
            mc = jnp.where(first[c] - t - 1 >= 0, jnp.max(worst), -jnp.inf)
            m = mc if m is None else jnp.maximum(m, mc)
        for c in range(Q_CHAINS):
            @pl.when(first[c] - t - 1 < 0)
            def _(c=c):
                for h in range(ATTN_HEADS):
                    r_scr[c * ATTN_HEADS + h] = jnp.full((BLK, BLK), -MASKED, F32)
        return t + 1, m

    def cond(carry):
        return carry[1] > LOG_W_ZERO

    t_end, _ = lax.while_loop(cond, body, (jnp.int32(0), jnp.float32(0.0)))
    for c in range(Q_CHAINS):
        for p in range(HEAD_PAIRS):
            apply_weights(first[c] - t_end + 1, c, p)

    for c in range(Q_CHAINS):
        o = jnp.concatenate([acc_scr[c * HEAD_PAIRS + p] for p in range(HEAD_PAIRS)], axis=1)
        o_ref[c * BLK:(c + 1) * BLK, :] = (_rms(o) * g_ref[...]).astype(BF16)


def _attention(q, k, v, g, cmat, batch, seq):
    n = q.shape[0]
    rows = Q_CHAINS * BLK
    steps = seq // rows
    qrow = lambda b, i: (b * steps + i, 0)
    kv = lambda b, i: (b, 0)
    fixed = lambda b, i: (0, 0)
    pairs = Q_CHAINS * HEAD_PAIRS
    return pl.pallas_call(
        _attn_kernel,
        grid=(batch, steps),
        in_specs=[pl.BlockSpec((rows, D_ATTN), qrow),
                  pl.BlockSpec((seq, 2 * D_ATTN), kv),
                  pl.BlockSpec((seq, 2 * D_ATTN), kv),
                  pl.BlockSpec((1, D_ATTN), fixed),
                  pl.BlockSpec((BLK, 2 * BLK), fixed)],
        out_specs=pl.BlockSpec((rows, D_ATTN), qrow),
        out_shape=jax.ShapeDtypeStruct((n, D_ATTN), BF16),
        scratch_shapes=[pltpu.VMEM((pairs, BLK, 2 * BLK), F32),
                        pltpu.VMEM((pairs, BLK, 2 * BLK), BF16),
                        pltpu.VMEM((Q_CHAINS * ATTN_HEADS, BLK, BLK), F32),
                        pltpu.VMEM((pairs, BLK, LANES), F32)],
        compiler_params=pltpu.CompilerParams(dimension_semantics=("arbitrary", "arbitrary"),
                                             vmem_limit_bytes=VMEM_LIMIT_BYTES),
        name="attention",
    )(q, k, v, g, cmat)


def _ssm_kernel(u_ref, bre_ref, bim_ref, lbre_ref, lbim_ref, cre_ref, cim_ref, d_ref,
                y_ref, utb_scr, ytb_scr, sr_scr, si_scr, *x_scr):
    xr_scr, xi_scr = x_scr[:SCAN_CHUNKS], x_scr[SCAN_CHUNKS:]

    @pl.when(pl.program_id(0) == 0)
    def _():
        sr_scr[...] = jnp.zeros_like(sr_scr)
        si_scr[...] = jnp.zeros_like(si_scr)

    for b in range(SUBLANES):
        for s in range(SSM_SLABS):
            utb_scr[s, pl.ds(b, SCAN_T, stride=SUBLANES), :] = u_ref[b, :, s * LANES:(s + 1) * LANES]

    slabs_per_slice = MXU_DIM // LANES
    chunks_per_slice = SCAN_CHUNKS // K_SLICES
    uk = [jnp.concatenate([utb_scr[k * slabs_per_slice + s] for s in range(slabs_per_slice)],
                          axis=1).astype(BF16) for k in range(K_SLICES)]

    def project_in(c):
        k, part = divmod(c, chunks_per_slice)
        cols = slice(part * SCAN_COLS, (part + 1) * SCAN_COLS)
        xr_scr[c][...] = jnp.dot(uk[k], bre_ref[k, :, cols], preferred_element_type=F32)
        xi_scr[c][...] = jnp.dot(uk[k], bim_ref[k, :, cols], preferred_element_type=F32)

    def recur(c):
        cols = slice(c * SCAN_COLS, (c + 1) * SCAN_COLS)
        lr = jnp.broadcast_to(lbre_ref[:, cols], (SUBLANES, SCAN_COLS))
        li = jnp.broadcast_to(lbim_ref[:, cols], (SUBLANES, SCAN_COLS))
        sr, si = sr_scr[:, cols], si_scr[:, cols]
        for t in range(SCAN_T):
            rows = slice(t * SUBLANES, (t + 1) * SUBLANES)
            sr, si = (lr * sr - li * si + xr_scr[c][rows, :],
                      lr * si + li * sr + xi_scr[c][rows, :])
            xr_scr[c][rows, :] = sr
            xi_scr[c][rows, :] = si
        sr_scr[:, cols] = sr
        si_scr[:, cols] = si

    def project_out(c):
        k, part = divmod(c, chunks_per_slice)
        rows = slice(part * SCAN_COLS, (part + 1) * SCAN_COLS)
        y = jnp.dot(xr_scr[c][...].astype(BF16), cre_ref[k, rows, :], preferred_element_type=F32)
        y = y - jnp.dot(xi_scr[c][...].astype(BF16), cim_ref[k, rows, :], preferred_element_type=F32)
        for s in range(slabs_per_slice):
            slab = k * slabs_per_slice + s
            lanes = slice(slab * LANES, (slab + 1) * LANES)
            ys = y[:, s * LANES:(s + 1) * LANES]
            if part == 0:
                ytb_scr[slab] = ys + d_ref[:, lanes] * utb_scr[slab]
            else:
                ytb_scr[slab] += ys

    for stage in range(SCAN_CHUNKS + 2):
        if stage < SCAN_CHUNKS:
            project_in(stage)
        if 1 <= stage <= SCAN_CHUNKS:
            recur(stage - 1)
        if stage >= 2:
            project_out(stage - 2)

    for b in range(SUBLANES):
        for s in range(SSM_SLABS):
            y_ref[b, :, s * LANES:(s + 1) * LANES] = ytb_scr[s, pl.ds(b, SCAN_T, stride=SUBLANES), :]


def _ssm_scan(u3, bre, bim, lbre, lbim, cre, cim, d):
    batch, seq, _ = u3.shape
    rows = SCAN_T * batch
    chunk = lambda i: (0, i, 0)
    fixed2 = lambda i: (0, 0)
    fixed3 = lambda i: (0, 0, 0)
    return pl.pallas_call(
        _ssm_kernel,
        grid=(seq // SCAN_T,),
        in_specs=[pl.BlockSpec((batch, SCAN_T, D_SSM), chunk),
                  pl.BlockSpec((K_SLICES, MXU_DIM, STATE_PER_SLICE), fixed3),
                  pl.BlockSpec((K_SLICES, MXU_DIM, STATE_PER_SLICE), fixed3),
                  pl.BlockSpec((1, N_STATE), fixed2),
                  pl.BlockSpec((1, N_STATE), fixed2),
                  pl.BlockSpec((K_SLICES, STATE_PER_SLICE, MXU_DIM), fixed3),
                  pl.BlockSpec((K_SLICES, STATE_PER_SLICE, MXU_DIM), fixed3),
                  pl.BlockSpec((1, D_SSM), fixed2)],
        out_specs=pl.BlockSpec((batch, SCAN_T, D_SSM), chunk),
        out_shape=jax.ShapeDtypeStruct((batch, seq, D_SSM), F32),
        scratch_shapes=[pltpu.VMEM((SSM_SLABS, rows, LANES), F32), pltpu.VMEM((SSM_SLABS, rows, LANES), F32),
                        pltpu.VMEM((SUBLANES, N_STATE), F32), pltpu.VMEM((SUBLANES, N_STATE), F32)]
        + [pltpu.VMEM((rows, SCAN_COLS), F32)] * (2 * SCAN_CHUNKS),
        compiler_params=pltpu.CompilerParams(dimension_semantics=("arbitrary",),
                                             vmem_limit_bytes=VMEM_LIMIT_BYTES),
        name="ssm_scan",
    )(u3, bre, bim, lbre, lbim, cre, cim, d)


def _post_kernel(x_ref, oa_ref, y_ref, wglu_ref, gs_ref, wout_ref, g2_ref,
                 wg_ref, wu_ref, wd_ref, gf_ref, o_ref):
    for s in range(TM_POST // SUB_ROWS):
        rows = slice(s * SUB_ROWS, (s + 1) * SUB_ROWS)
        yg = jax.nn.gelu(y_ref[rows, :])
        gate = jnp.dot(yg.astype(BF16), wglu_ref[...], preferred_element_type=F32)
        y2 = yg * jax.nn.sigmoid(gate)
        o_ssm = (_rms(y2) * gs_ref[...]).astype(BF16)
        x1 = x_ref[rows, :] + jnp.dot(oa_ref[rows, :], wout_ref[0:D_ATTN, :], preferred_element_type=F32)
        x1 = x1 + jnp.dot(o_ssm, wout_ref[D_ATTN:, :], preferred_element_type=F32)
        h2 = (_rms(x1) * g2_ref[...]).astype(BF16)
        o_ref[rows, :] = x1
        for c in range(D_FF // FF_CHUNK):
            cols = slice(c * FF_CHUNK, (c + 1) * FF_CHUNK)
            gt = jnp.dot(h2, wg_ref[:, cols], preferred_element_type=F32)
            up = jnp.dot(h2, wu_ref[:, cols], preferred_element_type=F32)
            a = (gt * jax.nn.sigmoid(gt) * up).astype(BF16)
            o_ref[rows, :] += jnp.dot(a, wd_ref[cols, :], preferred_element_type=F32)
        o_ref[rows, :] = _rms(o_ref[rows, :]) * gf_ref[...]


def _post(x2, oa, y, wglu, gs, wout, g2, wg, wu, wd, gf):
    n = x2.shape[0]
    row = lambda i: (i, 0)
    fixed = lambda i: (0, 0)
    once = functools.partial(pl.BlockSpec, index_map=fixed, pipeline_mode=pl.Buffered(1))
    return pl.pallas_call(
        _post_kernel,
        grid=(n // TM_POST,),
        in_specs=[pl.BlockSpec((TM_POST, D_MODEL), row),
                  pl.BlockSpec((TM_POST, D_ATTN), row),
                  pl.BlockSpec((TM_POST, D_SSM), row),
                  once((D_SSM, D_SSM)),
                  once((1, D_SSM)),
                  once((D_MODEL, D_MODEL)),
                  once((1, D_MODEL)),
                  once((D_MODEL, D_FF)),
                  once((D_MODEL, D_FF)),
                  once((D_FF, D_MODEL)),
                  once((1, D_MODEL))],
        out_specs=pl.BlockSpec((TM_POST, D_MODEL), row),
        out_shape=jax.ShapeDtypeStruct((n, D_MODEL), F32),
        compiler_params=pltpu.CompilerParams(dimension_semantics=("arbitrary",),
                                             vmem_limit_bytes=VMEM_LIMIT_BYTES),
        name="post",
    )(x2, oa, y, wglu, gs, wout, g2, wg, wu, wd, gf)


def _block_diag_b(bb):
    gl = MXU_DIM // SSM_GROUP
    eye = jnp.eye(gl, dtype=bb.dtype)
    t = bb.reshape(K_SLICES, gl, SSM_GROUP, SSM_STATE)
    m = t[:, :, :, None, :] * eye[None, :, None, :, None]
    return m.reshape(K_SLICES, MXU_DIM, STATE_PER_SLICE).astype(BF16)


def _block_diag_c(c):
    gl = MXU_DIM // SSM_GROUP
    eye = jnp.eye(gl, dtype=c.dtype)
    t = c.reshape(K_SLICES, gl, SSM_GROUP, SSM_STATE).transpose(0, 1, 3, 2)
    m = t[:, :, :, None, :] * eye[None, :, None, :, None]
    return m.reshape(K_SLICES, STATE_PER_SLICE, MXU_DIM).astype(BF16)


def _cumsum_matrix():
    j = jnp.arange(BLK)[:, None]
    s = jnp.arange(BLK)[None, :]
    tri = -(j >= s).astype(BF16)
    return jnp.concatenate([tri, -jnp.ones((BLK, BLK), BF16)], axis=1)


def kernel(x, norm1_g, w_in, attn_norm_g, lambda_re, lambda_im, log_step, b_re, b_im, c_re, c_im,
           d_skip, w_glu, ssm_norm_g, w_out, norm2_g, w_gate, w_up, w_down, final_norm_g):
    batch, seq, d_model = x.shape
    assert d_model == D_MODEL and norm1_g.shape[0] == 1 and batch == SUBLANES
    n = batch * seq
    x2 = x.reshape(n, D_MODEL)

    q, k, v, u = _in_proj(x2, norm1_g[0][None], w_in[0].astype(BF16))
    oa = _attention(q, k, v, attn_norm_g[0][None], _cumsum_matrix(), batch, seq)

    lb_re, lb_im, bb_re, bb_im = _ssm_prep(
        lambda_re[0], lambda_im[0], log_step[0][:, None],
        b_re[0].transpose(0, 2, 1), b_im[0].transpose(0, 2, 1))
    y = _ssm_scan(u.reshape(batch, seq, D_SSM), _block_diag_b(bb_re), _block_diag_b(bb_im),
                  lb_re.reshape(1, N_STATE), lb_im.reshape(1, N_STATE),
                  _block_diag_c(c_re[0]), _block_diag_c(c_im[0]), d_skip[0].reshape(1, D_SSM))

    out = _post(x2, oa, y.reshape(n, D_SSM), w_glu[0].astype(BF16), ssm_norm_g[0][None],
                w_out[0].astype(BF16), norm2_g[0][None], w_gate[0].astype(BF16),
                w_up[0].astype(BF16), w_down[0].astype(BF16), final_norm_g[None])
    return out.reshape(batch, seq, D_MODEL)
```

```python
import functools

import jax
import jax.numpy as jnp
from jax import lax
from jax.experimental import pallas as pl
from jax.experimental.pallas import tpu as pltpu

F32 = jnp.float32
BF16 = jnp.bfloat16

D_MODEL = 1024
ATTN_HEADS = 8
HEAD_DIM = 64
D_ATTN = ATTN_HEADS * HEAD_DIM
D_SSM = D_MODEL - D_ATTN
SSM_GROUP = 16
SSM_GROUPS = D_SSM // SSM_GROUP
SSM_STATE = 64
N_STATE = SSM_GROUPS * SSM_STATE
D_FF = 2816
D_IN = 3 * D_ATTN + D_SSM
EPS = 1e-6

LANES = 128
SUBLANES = 8
MXU_DIM = 256
VMEM_LIMIT_BYTES = 56 * 1024 * 1024

SUB_ROWS = 512
TM_PROJ = 2 * SUB_ROWS
TM_POST = 2 * SUB_ROWS
FF_CHUNK = 256
BLK = 128
HEAD_PAIRS = ATTN_HEADS // 2
Q_CHAINS = 8
SCAN_T = 64
SCAN_COLS = 512
SCAN_CHUNKS = N_STATE // SCAN_COLS
K_SLICES = D_SSM // MXU_DIM
STATE_PER_SLICE = N_STATE // K_SLICES
SSM_SLABS = D_SSM // LANES
LOG_W_ZERO = -104.0
MASKED = 1e30
LOG2_E = 1.4426950408889634


def _rms(x):
    return x * lax.rsqrt(jnp.mean(x * x, axis=-1, keepdims=True) + EPS)


def _ssm_prep_kernel(lre_ref, lim_ref, ls_ref, bre_ref, bim_ref,
                     lbre_ref, lbim_ref, bbre_ref, bbim_ref):
    lam_re = lre_ref[...]
    lam_im = lim_ref[...]
    dt = jnp.exp(ls_ref[...])
    mag = jnp.exp(lam_re * dt)
    ang = lam_im * dt
    lb_re = mag * jnp.cos(ang)
    lb_im = mag * jnp.sin(ang)
    den = lam_re * lam_re + lam_im * lam_im
    num_re = lb_re - 1.0
    f_re = (num_re * lam_re + lb_im * lam_im) / den
    f_im = (lb_im * lam_re - num_re * lam_im) / den
    lbre_ref[...] = lb_re
    lbim_ref[...] = lb_im
    br = bre_ref[...]
    bi = bim_ref[...]
    bbre_ref[...] = f_re[:, None, :] * br - f_im[:, None, :] * bi
    bbim_ref[...] = f_re[:, None, :] * bi + f_im[:, None, :] * br


def _ssm_prep(lambda_re, lambda_im, log_step, b_re_t, b_im_t):
    g, p = lambda_re.shape
    h = b_re_t.shape[1]
    return pl.pallas_call(
        _ssm_prep_kernel,
        out_shape=(jax.ShapeDtypeStruct((g, p), F32), jax.ShapeDtypeStruct((g, p), F32),
                   jax.ShapeDtypeStruct((g, h, p), F32), jax.ShapeDtypeStruct((g, h, p), F32)),
        name="ssm_prep",
    )(lambda_re, lambda_im, log_step, b_re_t, b_im_t)


def _in_proj_kernel(x_ref, g_ref, w32_ref, q_ref, k_ref, v_ref, u_ref, w_ref):
    @pl.when(pl.program_id(0) == 0)
    def _():
        w_ref[...] = w32_ref[...].astype(BF16)

    scale = HEAD_DIM ** -0.5
    low_half = lax.broadcasted_iota(jnp.int32, (SUB_ROWS, LANES), 1) < HEAD_DIM
    for s in range(TM_PROJ // SUB_ROWS):
        rows = slice(s * SUB_ROWS, (s + 1) * SUB_ROWS)
        h = (_rms(x_ref[rows, :]) * g_ref[...]).astype(BF16)
        q = jnp.dot(h, w_ref[:, 0:D_ATTN], preferred_element_type=F32)
        q_ref[rows, :] = (q * scale).astype(BF16)
        k = jnp.dot(h, w_ref[:, D_ATTN:2 * D_ATTN], preferred_element_type=F32)
        v = jnp.dot(h, w_ref[:, 2 * D_ATTN:3 * D_ATTN], preferred_element_type=F32)
        for full, ref in ((k, k_ref), (v, v_ref)):
            for p in range(HEAD_PAIRS):
                pair = full[:, p * LANES:(p + 1) * LANES].astype(BF16)
                zero = jnp.zeros_like(pair)
                ref[rows, (2 * p) * LANES:(2 * p + 1) * LANES] = jnp.where(low_half, pair, zero)
                ref[rows, (2 * p + 1) * LANES:(2 * p + 2) * LANES] = jnp.where(low_half, zero, pair)
        u_ref[rows, :] = jnp.dot(h, w_ref[:, 3 * D_ATTN:], preferred_element_type=F32)


def _in_proj(x2, g, w):
    n = x2.shape[0]
    row = lambda i: (i, 0)
    fixed = lambda i: (0, 0)
    return pl.pallas_call(
        _in_proj_kernel,
        grid=(n // TM_PROJ,),
        in_specs=[pl.BlockSpec((TM_PROJ, D_MODEL), row),
                  pl.BlockSpec((1, D_MODEL), fixed),
                  pl.BlockSpec((D_MODEL, D_IN), fixed, pipeline_mode=pl.Buffered(1))],
        out_specs=[pl.BlockSpec((TM_PROJ, D_ATTN), row), pl.BlockSpec((TM_PROJ, 2 * D_ATTN), row),
                   pl.BlockSpec((TM_PROJ, 2 * D_ATTN), row), pl.BlockSpec((TM_PROJ, D_SSM), row)],
        out_shape=[jax.ShapeDtypeStruct((n, D_ATTN), BF16), jax.ShapeDtypeStruct((n, 2 * D_ATTN), BF16),
                   jax.ShapeDtypeStruct((n, 2 * D_ATTN), BF16), jax.ShapeDtypeStruct((n, D_SSM), F32)],
        scratch_shapes=[pltpu.VMEM((D_MODEL, D_IN), BF16)],
        compiler_params=pltpu.CompilerParams(dimension_semantics=("arbitrary",),
                                             vmem_limit_bytes=VMEM_LIMIT_BYTES),
        name="in_proj",
    )(x2, g, w)


def _attn_kernel(q_ref, k_ref, v_ref, g_ref, cmat_ref, o_ref, z_scr, w_scr, r_scr, acc_scr):
    step = pl.program_id(1)
    row = lax.broadcasted_iota(jnp.int32, (BLK, BLK), 0)
    col = lax.broadcasted_iota(jnp.int32, (BLK, BLK), 1)
    before = col < row
    cmat = cmat_ref[...]
    nt_dims = (((1,), (1,)), ((), ()))
    first = [step * Q_CHAINS + c for c in range(Q_CHAINS)]

    def key_rows(j, c):
        return pl.ds(pl.multiple_of(jnp.clip(j, 0, first[c]) * BLK, BLK), BLK)

    def per_head_rows(ref, j, c, p):
        rows = key_rows(j, c)
        return jnp.concatenate([ref[rows, (2 * p) * LANES:(2 * p + 1) * LANES],
                                ref[rows, (2 * p + 1) * LANES:(2 * p + 2) * LANES]], axis=0)

    def scores(j, c, p):
        qp = q_ref[c * BLK:(c + 1) * BLK, p * LANES:(p + 1) * LANES]
        return lax.dot_general(qp, per_head_rows(k_ref, j, c, p), nt_dims,
                               preferred_element_type=F32)

    def apply_weights(j, c, p):
        u = c * HEAD_PAIRS + p
        acc_scr[u] += jnp.dot(w_scr[u], per_head_rows(v_ref, j, c, p), preferred_element_type=F32)

    for c in range(Q_CHAINS):
        for p in range(HEAD_PAIRS):
            z = scores(first[c], c, p)
            for hh in range(2):
                keys = slice(hh * BLK, (hh + 1) * BLK)
                z_scr[c * HEAD_PAIRS + p, :, keys] = jnp.where(before, z[:, keys], -MASKED)
    for u in range(Q_CHAINS * HEAD_PAIRS):
        acc_scr[u] = jnp.zeros((BLK, LANES), F32)
        w_scr[u] = jnp.zeros((BLK, 2 * BLK), BF16)
    for n in range(Q_CHAINS * ATTN_HEADS):
        r_scr[n] = jnp.zeros((BLK, BLK), F32)

    def body(carry):
        t, _ = carry
        m = None
        for c in range(Q_CHAINS):
            sps = []
            for h in range(ATTN_HEADS):
                z = z_scr[c * HEAD_PAIRS + h // 2, :, (h % 2) * BLK:(h % 2 + 1) * BLK]
                sp = jnp.maximum(z, 0.0) + jnp.log(1.0 + jnp.exp2(jnp.abs(z) * -LOG2_E))
                sps.append(sp.astype(BF16))
            cs = jnp.dot(jnp.concatenate(sps, axis=0), cmat,
                         preferred_element_type=F32)
            for p in range(HEAD_PAIRS):
                apply_weights(first[c] - t + 1, c, p)
            worst = None
            for h in range(ATTN_HEADS):
                n = c * ATTN_HEADS + h
                csn = cs[h * BLK:(h + 1) * BLK]
                r = r_scr[n]
                r_new = r + csn[:, BLK:]
                r_scr[n] = r_new
                worst = r_new if worst is None else jnp.maximum(worst, r_new)
                z = z_scr[c * HEAD_PAIRS + h // 2, :, (h % 2) * BLK:(h % 2 + 1) * BLK]
                w = jnp.exp2((z + csn[:, :BLK] + r) * LOG2_E)
                w_scr[c * HEAD_PAIRS + h // 2, :, (h % 2) * BLK:(h % 2 + 1) * BLK] = w.astype(BF16)
            for p in range(HEAD_PAIRS):
                z_scr[c * HEAD_PAIRS + p] = scores(first[c] - t - 1, c, p)
            mc = jnp.where(first[c] - t - 1 >= 0, jnp.max(worst), -jnp.inf)
            m = mc if m is None else jnp.maximum(m, mc)
        for c in range(Q_CHAINS):
            @pl.when(first[c] - t - 1 < 0)
            def _(c=c):
                for h in range(ATTN_HEADS):
                    r_scr[c * ATTN_HEADS + h] = jnp.full((BLK, BLK), -MASKED, F32)
        return t + 1, m

    def cond(carry):
        return carry[1] > LOG_W_ZERO

    t_end, _ = lax.while_loop(cond, body, (jnp.int32(0), jnp.float32(0.0)))
    for c in range(Q_CHAINS):
        for p in range(HEAD_PAIRS):
            apply_weights(first[c] - t_end + 1, c, p)

    for c in range(Q_CHAINS):
        o = jnp.concatenate([acc_scr[c * HEAD_PAIRS + p] for p in range(HEAD_PAIRS)], axis=1)
        o_ref[c * BLK:(c + 1) * BLK, :] = (_rms(o) * g_ref[...]).astype(BF16)


def _attention(q, k, v, g, cmat, batch, seq):
    n = q.shape[0]
    rows = Q_CHAINS * BLK
    steps = seq // rows
    qrow = lambda b, i: (b * steps + i, 0)
    kv = lambda b, i: (b, 0)
    fixed = lambda b, i: (0, 0)
    pairs = Q_CHAINS * HEAD_PAIRS
    return pl.pallas_call(
        _attn_kernel,
        grid=(batch, steps),
        in_specs=[pl.BlockSpec((rows, D_ATTN), qrow),
                  pl.BlockSpec((seq, 2 * D_ATTN), kv),
                  pl.BlockSpec((seq, 2 * D_ATTN), kv),
                  pl.BlockSpec((1, D_ATTN), fixed),
                  pl.BlockSpec((BLK, 2 * BLK), fixed)],
        out_specs=pl.BlockSpec((rows, D_ATTN), qrow),
        out_shape=jax.ShapeDtypeStruct((n, D_ATTN), BF16),
        scratch_shapes=[pltpu.VMEM((pairs, BLK, 2 * BLK), F32),
                        pltpu.VMEM((pairs, BLK, 2 * BLK), BF16),
                        pltpu.VMEM((Q_CHAINS * ATTN_HEADS, BLK, BLK), F32),
                        pltpu.VMEM((pairs, BLK, LANES), F32)],
        compiler_params=pltpu.CompilerParams(dimension_semantics=("arbitrary", "arbitrary"),
                                             vmem_limit_bytes=VMEM_LIMIT_BYTES),
        name="attention",
    )(q, k, v, g, cmat)


def _ssm_kernel(u_ref, bre_ref, bim_ref, lbre_ref, lbim_ref, cre_ref, cim_ref, d_ref,
                y_ref, utb_scr, ytb_scr, sr_scr, si_scr, *x_scr):
    xr_scr, xi_scr = x_scr[:SCAN_CHUNKS], x_scr[SCAN_CHUNKS:]

    @pl.when(pl.program_id(0) == 0)
    def _():
        sr_scr[...] = jnp.zeros_like(sr_scr)
        si_scr[...] = jnp.zeros_like(si_scr)

    for b in range(SUBLANES):
        for s in range(SSM_SLABS):
            utb_scr[s, pl.ds(b, SCAN_T, stride=SUBLANES), :] = u_ref[b, :, s * LANES:(s + 1) * LANES]

    slabs_per_slice = MXU_DIM // LANES
    chunks_per_slice = SCAN_CHUNKS // K_SLICES
    uk = [jnp.concatenate([utb_scr[k * slabs_per_slice + s] for s in range(slabs_per_slice)],
                          axis=1).astype(BF16) for k in range(K_SLICES)]

    def project_in(c):
        k, part = divmod(c, chunks_per_slice)
        cols = slice(part * SCAN_COLS, (part + 1) * SCAN_COLS)
        xr_scr[c][...] = jnp.dot(uk[k], bre_ref[k, :, cols], preferred_element_type=F32)
        xi_scr[c][...] = jnp.dot(uk[k], bim_ref[k, :, cols], preferred_element_type=F32)

    def recur(c):
        cols = slice(c * SCAN_COLS, (c + 1) * SCAN_COLS)
        lr = jnp.broadcast_to(lbre_ref[:, cols], (SUBLANES, SCAN_COLS))
        li = jnp.broadcast_to(lbim_ref[:, cols], (SUBLANES, SCAN_COLS))
        sr, si = sr_scr[:, cols], si_scr[:, cols]
        for t in range(SCAN_T):
            rows = slice(t * SUBLANES, (t + 1) * SUBLANES)
            sr, si = (lr * sr - li * si + xr_scr[c][rows, :],
                      lr * si + li * sr + xi_scr[c][rows, :])
            xr_scr[c][rows, :] = sr
            xi_scr[c][rows, :] = si
        sr_scr[:, cols] = sr
        si_scr[:, cols] = si

    def project_out(c):
        k, part = divmod(c, chunks_per_slice)
        rows = slice(part * SCAN_COLS, (part + 1) * SCAN_COLS)
        y = jnp.dot(xr_scr[c][...].astype(BF16), cre_ref[k, rows, :], preferred_element_type=F32)
        y = y - jnp.dot(xi_scr[c][...].astype(BF16), cim_ref[k, rows, :], preferred_element_type=F32)
        for s in range(slabs_per_slice):
            slab = k * slabs_per_slice + s
            lanes = slice(slab * LANES, (slab + 1) * LANES)
            ys = y[:, s * LANES:(s + 1) * LANES]
            if part == 0:
                ytb_scr[slab] = ys + d_ref[:, lanes] * utb_scr[slab]
            else:
                ytb_scr[slab] += ys

    for stage in range(SCAN_CHUNKS + 2):
        if stage < SCAN_CHUNKS:
            project_in(stage)
        if 1 <= stage <= SCAN_CHUNKS:
            recur(stage - 1)
        if stage >= 2:
            project_out(stage - 2)

    for b in range(SUBLANES):
        for s in range(SSM_SLABS):
            y_ref[b, :, s * LANES:(s + 1) * LANES] = ytb_scr[s, pl.ds(b, SCAN_T, stride=SUBLANES), :]


def _ssm_scan(u3, bre, bim, lbre, lbim, cre, cim, d):
    batch, seq, _ = u3.shape
    rows = SCAN_T * batch
    chunk = lambda i: (0, i, 0)
    fixed2 = lambda i: (0, 0)
    fixed3 = lambda i: (0, 0, 0)
    return pl.pallas_call(
        _ssm_kernel,
        grid=(seq // SCAN_T,),
        in_specs=[pl.BlockSpec((batch, SCAN_T, D_SSM), chunk),
                  pl.BlockSpec((K_SLICES, MXU_DIM, STATE_PER_SLICE), fixed3),
                  pl.BlockSpec((K_SLICES, MXU_DIM, STATE_PER_SLICE), fixed3),
                  pl.BlockSpec((1, N_STATE), fixed2),
                  pl.BlockSpec((1, N_STATE), fixed2),
                  pl.BlockSpec((K_SLICES, STATE_PER_SLICE, MXU_DIM), fixed3),
                  pl.BlockSpec((K_SLICES, STATE_PER_SLICE, MXU_DIM), fixed3),
                  pl.BlockSpec((1, D_SSM), fixed2)],
        out_specs=pl.BlockSpec((batch, SCAN_T, D_SSM), chunk),
        out_shape=jax.ShapeDtypeStruct((batch, seq, D_SSM), F32),
        scratch_shapes=[pltpu.VMEM((SSM_SLABS, rows, LANES), F32), pltpu.VMEM((SSM_SLABS, rows, LANES), F32),
                        pltpu.VMEM((SUBLANES, N_STATE), F32), pltpu.VMEM((SUBLANES, N_STATE), F32)]
        + [pltpu.VMEM((rows, SCAN_COLS), F32)] * (2 * SCAN_CHUNKS),
        compiler_params=pltpu.CompilerParams(dimension_semantics=("arbitrary",),
                                             vmem_limit_bytes=VMEM_LIMIT_BYTES),
        name="ssm_scan",
    )(u3, bre, bim, lbre, lbim, cre, cim, d)


def _post_kernel(x_ref, oa_ref, y_ref, wglu_ref, gs_ref, wout_ref, g2_ref,
                 wg_ref, wu_ref, wd_ref, gf_ref, o_ref):
    for s in range(TM_POST // SUB_ROWS):
        rows = slice(s * SUB_ROWS, (s + 1) * SUB_ROWS)
        yg = jax.nn.gelu(y_ref[rows, :])
        gate = jnp.dot(yg.astype(BF16), wglu_ref[...], preferred_element_type=F32)
        y2 = yg * jax.nn.sigmoid(gate)
        o_ssm = (_rms(y2) * gs_ref[...]).astype(BF16)
        x1 = x_ref[rows, :] + jnp.dot(oa_ref[rows, :], wout_ref[0:D_ATTN, :], preferred_element_type=F32)
        x1 = x1 + jnp.dot(o_ssm, wout_ref[D_ATTN:, :], preferred_element_type=F32)
        h2 = (_rms(x1) * g2_ref[...]).astype(BF16)
        o_ref[rows, :] = x1
        for c in range(D_FF // FF_CHUNK):
            cols = slice(c * FF_CHUNK, (c + 1) * FF_CHUNK)
            gt = jnp.dot(h2, wg_ref[:, cols], preferred_element_type=F32)
            up = jnp.dot(h2, wu_ref[:, cols], preferred_element_type=F32)
            a = (gt * jax.nn.sigmoid(gt) * up).astype(BF16)
            o_ref[rows, :] += jnp.dot(a, wd_ref[cols, :], preferred_element_type=F32)
        o_ref[rows, :] = _rms(o_ref[rows, :]) * gf_ref[...]


def _post(x2, oa, y, wglu, gs, wout, g2, wg, wu, wd, gf):
    n = x2.shape[0]
    row = lambda i: (i, 0)
    fixed = lambda i: (0, 0)
    once = functools.partial(pl.BlockSpec, index_map=fixed, pipeline_mode=pl.Buffered(1))
    return pl.pallas_call(
        _post_kernel,
        grid=(n // TM_POST,),
        in_specs=[pl.BlockSpec((TM_POST, D_MODEL), row),
                  pl.BlockSpec((TM_POST, D_ATTN), row),
                  pl.BlockSpec((TM_POST, D_SSM), row),
                  once((D_SSM, D_SSM)),
                  once((1, D_SSM)),
                  once((D_MODEL, D_MODEL)),
                  once((1, D_MODEL)),
                  once((D_MODEL, D_FF)),
                  once((D_MODEL, D_FF)),
                  once((D_FF, D_MODEL)),
                  once((1, D_MODEL))],
        out_specs=pl.BlockSpec((TM_POST, D_MODEL), row),
        out_shape=jax.ShapeDtypeStruct((n, D_MODEL), F32),
        compiler_params=pltpu.CompilerParams(dimension_semantics=("arbitrary",),
                                             vmem_limit_bytes=VMEM_LIMIT_BYTES),
        name="post",
    )(x2, oa, y, wglu, gs, wout, g2, wg, wu, wd, gf)


def _block_diag_b(bb):
    gl = MXU_DIM // SSM_GROUP
    eye = jnp.eye(gl, dtype=bb.dtype)
    t = bb.reshape(K_SLICES, gl, SSM_GROUP, SSM_STATE)
    m = t[:, :, :, None, :] * eye[None, :, None, :, None]
    return m.reshape(K_SLICES, MXU_DIM, STATE_PER_SLICE).astype(BF16)


def _block_diag_c(c):
    gl = MXU_DIM // SSM_GROUP
    eye = jnp.eye(gl, dtype=c.dtype)
    t = c.reshape(K_SLICES, gl, SSM_GROUP, SSM_STATE).transpose(0, 1, 3, 2)
    m = t[:, :, :, None, :] * eye[None, :, None, :, None]
    return m.reshape(K_SLICES, STATE_PER_SLICE, MXU_DIM).astype(BF16)


def _cumsum_matrix():
    j = jnp.arange(BLK)[:, None]
    s = jnp.arange(BLK)[None, :]
    tri = -(j >= s).astype(BF16)
    return jnp.concatenate([tri, -jnp.ones((BLK, BLK), BF16)], axis=1)


def kernel(x, norm1_g, w_in, attn_norm_g, lambda_re, lambda_im, log_step, b_re, b_im, c_re, c_im,
           d_skip, w_glu, ssm_norm_g, w_out, norm2_g, w_gate, w_up, w_down, final_norm_g):
    batch, seq, d_model = x.shape
    assert d_model == D_MODEL and norm1_g.shape[0] == 1 and batch == SUBLANES
    n = batch * seq
    x2 = x.reshape(n, D_MODEL)

    q, k, v, u = _in_proj(x2, norm1_g[0][None], w_in[0])
    oa = _attention(q, k, v, attn_norm_g[0][None], _cumsum_matrix(), batch, seq)

    lb_re, lb_im, bb_re, bb_im = _ssm_prep(
        lambda_re[0], lambda_im[0], log_step[0][:, None],
        b_re[0].transpose(0, 2, 1), b_im[0].transpose(0, 2, 1))
    y = _ssm_scan(u.reshape(batch, seq, D_SSM), _block_diag_b(bb_re), _block_diag_b(bb_im),
                  lb_re.reshape(1, N_STATE), lb_im.reshape(1, N_STATE),
                  _block_diag_c(c_re[0]), _block_diag_c(c_im[0]), d_skip[0].reshape(1, D_SSM))

    out = _post(x2, oa, y.reshape(n, D_SSM), w_glu[0].astype(BF16), ssm_norm_g[0][None],
                w_out[0].astype(BF16), norm2_g[0][None], w_gate[0].astype(BF16),
                w_up[0].astype(BF16), w_down[0].astype(BF16), final_norm_g[None])
    return out.reshape(batch, seq, D_MODEL)
```

```python
import functools

import jax
import jax.numpy as jnp
from jax import lax
from jax.experimental import pallas as pl
from jax.experimental.pallas import tpu as pltpu

F32 = jnp.float32
BF16 = jnp.bfloat16

D_MODEL = 1024
ATTN_HEADS = 8
HEAD_DIM = 64
D_ATTN = ATTN_HEADS * HEAD_DIM
D_SSM = D_MODEL - D_ATTN
SSM_GROUP = 16
SSM_GROUPS = D_SSM // SSM_GROUP
SSM_STATE = 64
N_STATE = SSM_GROUPS * SSM_STATE
D_FF = 2816
D_IN = 3 * D_ATTN + D_SSM
EPS = 1e-6

LANES = 128
SUBLANES = 8
MXU_DIM = 256
VMEM_LIMIT_BYTES = 56 * 1024 * 1024

SUB_ROWS = 512
TM_PROJ = 2 * SUB_ROWS
TM_POST = 2 * SUB_ROWS
FF_CHUNK = 256
BLK = 128
HEAD_PAIRS = ATTN_HEADS // 2
Q_CHAINS = 8
SCAN_T = 128
SCAN_COLS = 512
SCAN_CHUNKS = N_STATE // SCAN_COLS
K_SLICES = D_SSM // MXU_DIM
STATE_PER_SLICE = N_STATE // K_SLICES
SSM_SLABS = D_SSM // LANES
LOG_W_ZERO = -104.0
MASKED = 1e30
LOG2_E = 1.4426950408889634


def _rms(x):
    return x * lax.rsqrt(jnp.mean(x * x, axis=-1, keepdims=True) + EPS)


def _ssm_prep_kernel(lre_ref, lim_ref, ls_ref, bre_ref, bim_ref,
                     lbre_ref, lbim_ref, bbre_ref, bbim_ref):
    lam_re = lre_ref[...]
    lam_im = lim_ref[...]
    dt = jnp.exp(ls_ref[...])
    mag = jnp.exp(lam_re * dt)
    ang = lam_im * dt
    lb_re = mag * jnp.cos(ang)
    lb_im = mag * jnp.sin(ang)
    den = lam_re * lam_re + lam_im * lam_im
    num_re = lb_re - 1.0
    f_re = (num_re * lam_re + lb_im * lam_im) / den
    f_im = (lb_im * lam_re - num_re * lam_im) / den
    lbre_ref[...] = lb_re
    lbim_ref[...] = lb_im
    br = bre_ref[...]
    bi = bim_ref[...]
    bbre_ref[...] = f_re[:, None, :] * br - f_im[:, None, :] * bi
    bbim_ref[...] = f_re[:, None, :] * bi + f_im[:, None, :] * br


def _ssm_prep(lambda_re, lambda_im, log_step, b_re_t, b_im_t):
    g, p = lambda_re.shape
    h = b_re_t.shape[1]
    return pl.pallas_call(
        _ssm_prep_kernel,
        out_shape=(jax.ShapeDtypeStruct((g, p), F32), jax.ShapeDtypeStruct((g, p), F32),
                   jax.ShapeDtypeStruct((g, h, p), F32), jax.ShapeDtypeStruct((g, h, p), F32)),
        name="ssm_prep",
    )(lambda_re, lambda_im, log_step, b_re_t, b_im_t)


def _in_proj_kernel(x_ref, g_ref, w32_ref, q_ref, k_ref, v_ref, u_ref, w_ref):
    @pl.when(pl.program_id(0) == 0)
    def _():
        w_ref[...] = w32_ref[...].astype(BF16)

    scale = HEAD_DIM ** -0.5
    low_half = lax.broadcasted_iota(jnp.int32, (SUB_ROWS, LANES), 1) < HEAD_DIM
    for s in range(TM_PROJ // SUB_ROWS):
        rows = slice(s * SUB_ROWS, (s + 1) * SUB_ROWS)
        h = (_rms(x_ref[rows, :]) * g_ref[...]).astype(BF16)
        q = jnp.dot(h, w_ref[:, 0:D_ATTN], preferred_element_type=F32)
        q_ref[rows, :] = (q * scale).astype(BF16)
        k = jnp.dot(h, w_ref[:, D_ATTN:2 * D_ATTN], preferred_element_type=F32)
        v = jnp.dot(h, w_ref[:, 2 * D_ATTN:3 * D_ATTN], preferred_element_type=F32)
        for full, ref in ((k, k_ref), (v, v_ref)):
            for p in range(HEAD_PAIRS):
                pair = full[:, p * LANES:(p + 1) * LANES].astype(BF16)
                zero = jnp.zeros_like(pair)
                ref[rows, (2 * p) * LANES:(2 * p + 1) * LANES] = jnp.where(low_half, pair, zero)
                ref[rows, (2 * p + 1) * LANES:(2 * p + 2) * LANES] = jnp.where(low_half, zero, pair)
        u_ref[rows, :] = jnp.dot(h, w_ref[:, 3 * D_ATTN:], preferred_element_type=F32)


def _in_proj(x2, g, w):
    n = x2.shape[0]
    row = lambda i: (i, 0)
    fixed = lambda i: (0, 0)
    return pl.pallas_call(
        _in_proj_kernel,
        grid=(n // TM_PROJ,),
        in_specs=[pl.BlockSpec((TM_PROJ, D_MODEL), row),
                  pl.BlockSpec((1, D_MODEL), fixed),
                  pl.BlockSpec((D_MODEL, D_IN), fixed, pipeline_mode=pl.Buffered(1))],
        out_specs=[pl.BlockSpec((TM_PROJ, D_ATTN), row), pl.BlockSpec((TM_PROJ, 2 * D_ATTN), row),
                   pl.BlockSpec((TM_PROJ, 2 * D_ATTN), row), pl.BlockSpec((TM_PROJ, D_SSM), row)],
        out_shape=[jax.ShapeDtypeStruct((n, D_ATTN), BF16), jax.ShapeDtypeStruct((n, 2 * D_ATTN), BF16),
                   jax.ShapeDtypeStruct((n, 2 * D_ATTN), BF16), jax.ShapeDtypeStruct((n, D_SSM), F32)],
        scratch_shapes=[pltpu.VMEM((D_MODEL, D_IN), BF16)],
        compiler_params=pltpu.CompilerParams(dimension_semantics=("arbitrary",),
                                             vmem_limit_bytes=VMEM_LIMIT_BYTES),
        name="in_proj",
    )(x2, g, w)


def _attn_kernel(q_ref, k_ref, v_ref, g_ref, cmat_ref, o_ref, z_scr, w_scr, r_scr, acc_scr):
    step = pl.program_id(1)
    row = lax.broadcasted_iota(jnp.int32, (BLK, BLK), 0)
    col = lax.broadcasted_iota(jnp.int32, (BLK, BLK), 1)
    before = col < row
    cmat = cmat_ref[...]
    nt_dims = (((1,), (1,)), ((), ()))
    first = [step * Q_CHAINS + c for c in range(Q_CHAINS)]

    def key_rows(j, c):
        return pl.ds(pl.multiple_of(jnp.clip(j, 0, first[c]) * BLK, BLK), BLK)

    def per_head_rows(ref, j, c, p):
        rows = key_rows(j, c)
        return jnp.concatenate([ref[rows, (2 * p) * LANES:(2 * p + 1) * LANES],
                                ref[rows, (2 * p + 1) * LANES:(2 * p + 2) * LANES]], axis=0)

    def scores(j, c, p):
        qp = q_ref[c * BLK:(c + 1) * BLK, p * LANES:(p + 1) * LANES]
        return lax.dot_general(qp, per_head_rows(k_ref, j, c, p), nt_dims,
                               preferred_element_type=F32)

    def apply_weights(j, c, p):
        u = c * HEAD_PAIRS + p
        acc_scr[u] += jnp.dot(w_scr[u], per_head_rows(v_ref, j, c, p), preferred_element_type=F32)

    for c in range(Q_CHAINS):
        for p in range(HEAD_PAIRS):
            z = scores(first[c], c, p)
            for hh in range(2):
                keys = slice(hh * BLK, (hh + 1) * BLK)
                z_scr[c * HEAD_PAIRS + p, :, keys] = jnp.where(before, z[:, keys], -MASKED)
    for u in range(Q_CHAINS * HEAD_PAIRS):
        acc_scr[u] = jnp.zeros((BLK, LANES), F32)
        w_scr[u] = jnp.zeros((BLK, 2 * BLK), BF16)
    for n in range(Q_CHAINS * ATTN_HEADS):
        r_scr[n] = jnp.zeros((BLK, BLK), F32)

    def body(carry):
        t, _ = carry
        m = None
        for c in range(Q_CHAINS):
            sps = []
            for h in range(ATTN_HEADS):
                z = z_scr[c * HEAD_PAIRS + h // 2, :, (h % 2) * BLK:(h % 2 + 1) * BLK]
                sp = jnp.maximum(z, 0.0) + jnp.log(1.0 + jnp.exp2(jnp.abs(z) * -LOG2_E))
                sps.append(sp.astype(BF16))
            cs = jnp.dot(jnp.concatenate(sps, axis=0), cmat,
                         preferred_element_type=F32)
            for p in range(HEAD_PAIRS):
                apply_weights(first[c] - t + 1, c, p)
            worst = None
            for h in range(ATTN_HEADS):
                n = c * ATTN_HEADS + h
                csn = cs[h * BLK:(h + 1) * BLK]
                r = r_scr[n]
                r_new = r + csn[:, BLK:]
                r_scr[n] = r_new
                worst = r_new if worst is None else jnp.maximum(worst, r_new)
                z = z_scr[c * HEAD_PAIRS + h // 2, :, (h % 2) * BLK:(h % 2 + 1) * BLK]
                w = jnp.exp2((z + csn[:, :BLK] + r) * LOG2_E)
                w_scr[c * HEAD_PAIRS + h // 2, :, (h % 2) * BLK:(h % 2 + 1) * BLK] = w.astype(BF16)
            for p in range(HEAD_PAIRS):
                z_scr[c * HEAD_PAIRS + p] = scores(first[c] - t - 1, c, p)
            mc = jnp.where(first[c] - t - 1 >= 0, jnp.max(worst), -jnp.inf)
            m = mc if m is None else jnp.maximum(m, mc)
        for c in range(Q_CHAINS):
            @pl.when(first[c] - t - 1 < 0)
            def _(c=c):
                for h in range(ATTN_HEADS):
                    r_scr[c * ATTN_HEADS + h] = jnp.full((BLK, BLK), -MASKED, F32)
        return t + 1, m

    def cond(carry):
        return carry[1] > LOG_W_ZERO

    t_end, _ = lax.while_loop(cond, body, (jnp.int32(0), jnp.float32(0.0)))
    for c in range(Q_CHAINS):
        for p in range(HEAD_PAIRS):
            apply_weights(first[c] - t_end + 1, c, p)

    for c in range(Q_CHAINS):
        o = jnp.concatenate([acc_scr[c * HEAD_PAIRS + p] for p in range(HEAD_PAIRS)], axis=1)
        o_ref[c * BLK:(c + 1) * BLK, :] = (_rms(o) * g_ref[...]).astype(BF16)


def _attention(q, k, v, g, cmat, batch, seq):
    n = q.shape[0]
    rows = Q_CHAINS * BLK
    steps = seq // rows
    qrow = lambda b, i: (b * steps + i, 0)
    kv = lambda b, i: (b, 0)
    fixed = lambda b, i: (0, 0)
    pairs = Q_CHAINS * HEAD_PAIRS
    return pl.pallas_call(
        _attn_kernel,
        grid=(batch, steps),
        in_specs=[pl.BlockSpec((rows, D_ATTN), qrow),
                  pl.BlockSpec((seq, 2 * D_ATTN), kv),
                  pl.BlockSpec((seq, 2 * D_ATTN), kv),
                  pl.BlockSpec((1, D_ATTN), fixed),
                  pl.BlockSpec((BLK, 2 * BLK), fixed)],
        out_specs=pl.BlockSpec((rows, D_ATTN), qrow),
        out_shape=jax.ShapeDtypeStruct((n, D_ATTN), BF16),
        scratch_shapes=[pltpu.VMEM((pairs, BLK, 2 * BLK), F32),
                        pltpu.VMEM((pairs, BLK, 2 * BLK), BF16),
                        pltpu.VMEM((Q_CHAINS * ATTN_HEADS, BLK, BLK), F32),
                        pltpu.VMEM((pairs, BLK, LANES), F32)],
        compiler_params=pltpu.CompilerParams(dimension_semantics=("arbitrary", "arbitrary"),
                                             vmem_limit_bytes=VMEM_LIMIT_BYTES),
        name="attention",
    )(q, k, v, g, cmat)


def _ssm_kernel(u_ref, bre_ref, bim_ref, lbre_ref, lbim_ref, cre_ref, cim_ref, d_ref,
                y_ref, utb_scr, ytb_scr, sr_scr, si_scr, *x_scr):
    xr_scr, xi_scr = x_scr[:SCAN_CHUNKS], x_scr[SCAN_CHUNKS:]

    @pl.when(pl.program_id(0) == 0)
    def _():
        sr_scr[...] = jnp.zeros_like(sr_scr)
        si_scr[...] = jnp.zeros_like(si_scr)

    for b in range(SUBLANES):
        for s in range(SSM_SLABS):
            utb_scr[s, pl.ds(b, SCAN_T, stride=SUBLANES), :] = u_ref[b, :, s * LANES:(s + 1) * LANES]

    slabs_per_slice = MXU_DIM // LANES
    chunks_per_slice = SCAN_CHUNKS // K_SLICES
    uk = [jnp.concatenate([utb_scr[k * slabs_per_slice + s] for s in range(slabs_per_slice)],
                          axis=1).astype(BF16) for k in range(K_SLICES)]

    def project_in(c):
        k, part = divmod(c, chunks_per_slice)
        cols = slice(part * SCAN_COLS, (part + 1) * SCAN_COLS)
        xr_scr[c][...] = jnp.dot(uk[k], bre_ref[k, :, cols], preferred_element_type=F32)
        xi_scr[c][...] = jnp.dot(uk[k], bim_ref[k, :, cols], preferred_element_type=F32)

    def recur(c):
        cols = slice(c * SCAN_COLS, (c + 1) * SCAN_COLS)
        lr = jnp.broadcast_to(lbre_ref[:, cols], (SUBLANES, SCAN_COLS))
        li = jnp.broadcast_to(lbim_ref[:, cols], (SUBLANES, SCAN_COLS))
        sr, si = sr_scr[:, cols], si_scr[:, cols]
        for t in range(SCAN_T):
            rows = slice(t * SUBLANES, (t + 1) * SUBLANES)
            sr, si = (lr * sr - li * si + xr_scr[c][rows, :],
                      lr * si + li * sr + xi_scr[c][rows, :])
            xr_scr[c][rows, :] = sr
            xi_scr[c][rows, :] = si
        sr_scr[:, cols] = sr
        si_scr[:, cols] = si

    def project_out(c):
        k, part = divmod(c, chunks_per_slice)
        rows = slice(part * SCAN_COLS, (part + 1) * SCAN_COLS)
        y = jnp.dot(xr_scr[c][...].astype(BF16), cre_ref[k, rows, :], preferred_element_type=F32)
        y = y - jnp.dot(xi_scr[c][...].astype(BF16), cim_ref[k, rows, :], preferred_element_type=F32)
        for s in range(slabs_per_slice):
            slab = k * slabs_per_slice + s
            lanes = slice(slab * LANES, (slab + 1) * LANES)
            ys = y[:, s * LANES:(s + 1) * LANES]
            if part == 0:
                ytb_scr[slab] = ys + d_ref[:, lanes] * utb_scr[slab]
            else:
                ytb_scr[slab] += ys

    for stage in range(SCAN_CHUNKS + 2):
        if stage < SCAN_CHUNKS:
            project_in(stage)
        if 1 <= stage <= SCAN_CHUNKS:
            recur(stage - 1)
        if stage >= 2:
            project_out(stage - 2)

    for b in range(SUBLANES):
        for s in range(SSM_SLABS):
            y_ref[b, :, s * LANES:(s + 1) * LANES] = ytb_scr[s, pl.ds(b, SCAN_T, stride=SUBLANES), :]


def _ssm_scan(u3, bre, bim, lbre, lbim, cre, cim, d):
    batch, seq, _ = u3.shape
    rows = SCAN_T * batch
    chunk = lambda i: (0, i, 0)
    fixed2 = lambda i: (0, 0)
    fixed3 = lambda i: (0, 0, 0)
    return pl.pallas_call(
        _ssm_kernel,
        grid=(seq // SCAN_T,),
        in_specs=[pl.BlockSpec((batch, SCAN_T, D_SSM), chunk),
                  pl.BlockSpec((K_SLICES, MXU_DIM, STATE_PER_SLICE), fixed3),
                  pl.BlockSpec((K_SLICES, MXU_DIM, STATE_PER_SLICE), fixed3),
                  pl.BlockSpec((1, N_STATE), fixed2),
                  pl.BlockSpec((1, N_STATE), fixed2),
                  pl.BlockSpec((K_SLICES, STATE_PER_SLICE, MXU_DIM), fixed3),
                  pl.BlockSpec((K_SLICES, STATE_PER_SLICE, MXU_DIM), fixed3),
                  pl.BlockSpec((1, D_SSM), fixed2)],
        out_specs=pl.BlockSpec((batch, SCAN_T, D_SSM), chunk),
        out_shape=jax.ShapeDtypeStruct((batch, seq, D_SSM), F32),
        scratch_shapes=[pltpu.VMEM((SSM_SLABS, rows, LANES), F32), pltpu.VMEM((SSM_SLABS, rows, LANES), F32),
                        pltpu.VMEM((SUBLANES, N_STATE), F32), pltpu.VMEM((SUBLANES, N_STATE), F32)]
        + [pltpu.VMEM((rows, SCAN_COLS), F32)] * (2 * SCAN_CHUNKS),
        compiler_params=pltpu.CompilerParams(dimension_semantics=("arbitrary",),
                                             vmem_limit_bytes=VMEM_LIMIT_BYTES),
        name="ssm_scan",
    )(u3, bre, bim, lbre, lbim, cre, cim, d)


def _post_kernel(x_ref, oa_ref, y_ref, wglu_ref, gs_ref, wout_ref, g2_ref,
                 wg_ref, wu_ref, wd_ref, gf_ref, o_ref):
    for s in range(TM_POST // SUB_ROWS):
        rows = slice(s * SUB_ROWS, (s + 1) * SUB_ROWS)
        yg = jax.nn.gelu(y_ref[rows, :])
        gate = jnp.dot(yg.astype(BF16), wglu_ref[...], preferred_element_type=F32)
        y2 = yg * jax.nn.sigmoid(gate)
        o_ssm = (_rms(y2) * gs_ref[...]).astype(BF16)
        x1 = x_ref[rows, :] + jnp.dot(oa_ref[rows, :], wout_ref[0:D_ATTN, :], preferred_element_type=F32)
        x1 = x1 + jnp.dot(o_ssm, wout_ref[D_ATTN:, :], preferred_element_type=F32)
        h2 = (_rms(x1) * g2_ref[...]).astype(BF16)
        o_ref[rows, :] = x1
        for c in range(D_FF // FF_CHUNK):
            cols = slice(c * FF_CHUNK, (c + 1) * FF_CHUNK)
            gt = jnp.dot(h2, wg_ref[:, cols], preferred_element_type=F32)
            up = jnp.dot(h2, wu_ref[:, cols], preferred_element_type=F32)
            a = (gt * jax.nn.sigmoid(gt) * up).astype(BF16)
            o_ref[rows, :] += jnp.dot(a, wd_ref[cols, :], preferred_element_type=F32)
        o_ref[rows, :] = _rms(o_ref[rows, :]) * gf_ref[...]


def _post(x2, oa, y, wglu, gs, wout, g2, wg, wu, wd, gf):
    n = x2.shape[0]
    row = lambda i: (i, 0)
    fixed = lambda i: (0, 0)
    once = functools.partial(pl.BlockSpec, index_map=fixed, pipeline_mode=pl.Buffered(1))
    return pl.pallas_call(
        _post_kernel,
        grid=(n // TM_POST,),
        in_specs=[pl.BlockSpec((TM_POST, D_MODEL), row),
                  pl.BlockSpec((TM_POST, D_ATTN), row),
                  pl.BlockSpec((TM_POST, D_SSM), row),
                  once((D_SSM, D_SSM)),
                  once((1, D_SSM)),
                  once((D_MODEL, D_MODEL)),
                  once((1, D_MODEL)),
                  once((D_MODEL, D_FF)),
                  once((D_MODEL, D_FF)),
                  once((D_FF, D_MODEL)),
                  once((1, D_MODEL))],
        out_specs=pl.BlockSpec((TM_POST, D_MODEL), row),
        out_shape=jax.ShapeDtypeStruct((n, D_MODEL), F32),
        compiler_params=pltpu.CompilerParams(dimension_semantics=("arbitrary",),
                                             vmem_limit_bytes=VMEM_LIMIT_BYTES),
        name="post",
    )(x2, oa, y, wglu, gs, wout, g2, wg, wu, wd, gf)


def _block_diag_b(bb):
    gl = MXU_DIM // SSM_GROUP
    eye = jnp.eye(gl, dtype=bb.dtype)
    t = bb.reshape(K_SLICES, gl, SSM_GROUP, SSM_STATE)
    m = t[:, :, :, None, :] * eye[None, :, None, :, None]
    return m.reshape(K_SLICES, MXU_DIM, STATE_PER_SLICE).astype(BF16)


def _block_diag_c(c):
    gl = MXU_DIM // SSM_GROUP
    eye = jnp.eye(gl, dtype=c.dtype)
    t = c.reshape(K_SLICES, gl, SSM_GROUP, SSM_STATE).transpose(0, 1, 3, 2)
    m = t[:, :, :, None, :] * eye[None, :, None, :, None]
    return m.reshape(K_SLICES, STATE_PER_SLICE, MXU_DIM).astype(BF16)


def _cumsum_matrix():
    j = jnp.arange(BLK)[:, None]
    s = jnp.arange(BLK)[None, :]
    tri = -(j >= s).astype(BF16)
    return jnp.concatenate([tri, -jnp.ones((BLK, BLK), BF16)], axis=1)


def kernel(x, norm1_g, w_in, attn_norm_g, lambda_re, lambda_im, log_step, b_re, b_im, c_re, c_im,
           d_skip, w_glu, ssm_norm_g, w_out, norm2_g, w_gate, w_up, w_down, final_norm_g):
    batch, seq, d_model = x.shape
    assert d_model == D_MODEL and norm1_g.shape[0] == 1 and batch == SUBLANES
    n = batch * seq
    x2 = x.reshape(n, D_MODEL)

    q, k, v, u = _in_proj(x2, norm1_g[0][None], w_in[0])
    oa = _attention(q, k, v, attn_norm_g[0][None], _cumsum_matrix(), batch, seq)

    lb_re, lb_im, bb_re, bb_im = _ssm_prep(
        lambda_re[0], lambda_im[0], log_step[0][:, None],
        b_re[0].transpose(0, 2, 1), b_im[0].transpose(0, 2, 1))
    y = _ssm_scan(u.reshape(batch, seq, D_SSM), _block_diag_b(bb_re), _block_diag_b(bb_im),
                  lb_re.reshape(1, N_STATE), lb_im.reshape(1, N_STATE),
                  _block_diag_c(c_re[0]), _block_diag_c(c_im[0]), d_skip[0].reshape(1, D_SSM))

    out = _post(x2, oa, y.reshape(n, D_SSM), w_glu[0].astype(BF16), ssm_norm_g[0][None],
                w_out[0].astype(BF16), norm2_g[0][None], w_gate[0].astype(BF16),
                w_up[0].astype(BF16), w_down[0].astype(BF16), final_norm_g[None])
    return out.reshape(batch, seq, D_MODEL)
```

```python
import functools

import jax
import jax.numpy as jnp
from jax import lax
from jax.experimental import pallas as pl
from jax.experimental.pallas import tpu as pltpu

F32 = jnp.float32
BF16 = jnp.bfloat16

D_MODEL = 1024
ATTN_HEADS = 8
HEAD_DIM = 64
D_ATTN = ATTN_HEADS * HEAD_DIM
D_SSM = D_MODEL - D_ATTN
SSM_GROUP = 16
SSM_GROUPS = D_SSM // SSM_GROUP
SSM_STATE = 64
N_STATE = SSM_GROUPS * SSM_STATE
D_FF = 2816
D_IN = 3 * D_ATTN + D_SSM
EPS = 1e-6

LANES = 128
SUBLANES = 8
MXU_DIM = 256
VMEM_LIMIT_BYTES = 56 * 1024 * 1024

SUB_ROWS = 512
TM_PROJ = 2 * SUB_ROWS
TM_POST = 2 * SUB_ROWS
FF_CHUNK = 256
BLK = 128
HEAD_PAIRS = ATTN_HEADS // 2
Q_CHAINS = 8
SCAN_T = 128
SCAN_COLS = 512
SCAN_CHUNKS = N_STATE // SCAN_COLS
K_SLICES = D_SSM // MXU_DIM
STATE_PER_SLICE = N_STATE // K_SLICES
SSM_SLABS = D_SSM // LANES
LOG_W_ZERO = -104.0
MASKED = 1e30
LOG2_E = 1.4426950408889634


def _rms(x):
    return x * lax.rsqrt(jnp.mean(x * x, axis=-1, keepdims=True) + EPS)


def _ssm_prep_kernel(lre_ref, lim_ref, ls_ref, bre_ref, bim_ref,
                     lbre_ref, lbim_ref, bbre_ref, bbim_ref):
    lam_re = lre_ref[...]
    lam_im = lim_ref[...]
    dt = jnp.exp(ls_ref[...])
    mag = jnp.exp(lam_re * dt)
    ang = lam_im * dt
    lb_re = mag * jnp.cos(ang)
    lb_im = mag * jnp.sin(ang)
    den = lam_re * lam_re + lam_im * lam_im
    num_re = lb_re - 1.0
    f_re = (num_re * lam_re + lb_im * lam_im) / den
    f_im = (lb_im * lam_re - num_re * lam_im) / den
    lbre_ref[...] = lb_re
    lbim_ref[...] = lb_im
    br = bre_ref[...]
    bi = bim_ref[...]
    bbre_ref[...] = f_re[:, None, :] * br - f_im[:, None, :] * bi
    bbim_ref[...] = f_re[:, None, :] * bi + f_im[:, None, :] * br


def _ssm_prep(lambda_re, lambda_im, log_step, b_re_t, b_im_t):
    g, p = lambda_re.shape
    h = b_re_t.shape[1]
    return pl.pallas_call(
        _ssm_prep_kernel,
        out_shape=(jax.ShapeDtypeStruct((g, p), F32), jax.ShapeDtypeStruct((g, p), F32),
                   jax.ShapeDtypeStruct((g, h, p), F32), jax.ShapeDtypeStruct((g, h, p), F32)),
        name="ssm_prep",
    )(lambda_re, lambda_im, log_step, b_re_t, b_im_t)


def _in_proj_kernel(x_ref, g_ref, w32_ref, q_ref, k_ref, v_ref, u_ref, w_ref):
    @pl.when(pl.program_id(0) == 0)
    def _():
        w_ref[...] = w32_ref[...].astype(BF16)

    scale = HEAD_DIM ** -0.5
    low_half = lax.broadcasted_iota(jnp.int32, (SUB_ROWS, LANES), 1) < HEAD_DIM
    for s in range(TM_PROJ // SUB_ROWS):
        rows = slice(s * SUB_ROWS, (s + 1) * SUB_ROWS)
        h = (_rms(x_ref[rows, :]) * g_ref[...]).astype(BF16)
        q = jnp.dot(h, w_ref[:, 0:D_ATTN], preferred_element_type=F32)
        q_ref[rows, :] = (q * scale).astype(BF16)
        k = jnp.dot(h, w_ref[:, D_ATTN:2 * D_ATTN], preferred_element_type=F32)
        v = jnp.dot(h, w_ref[:, 2 * D_ATTN:3 * D_ATTN], preferred_element_type=F32)
        for full, ref in ((k, k_ref), (v, v_ref)):
            for p in range(HEAD_PAIRS):
                pair = full[:, p * LANES:(p + 1) * LANES].astype(BF16)
                zero = jnp.zeros_like(pair)
                ref[rows, (2 * p) * LANES:(2 * p + 1) * LANES] = jnp.where(low_half, pair, zero)
                ref[rows, (2 * p + 1) * LANES:(2 * p + 2) * LANES] = jnp.where(low_half, zero, pair)
        u_ref[rows, :] = jnp.dot(h, w_ref[:, 3 * D_ATTN:], preferred_element_type=F32)


def _in_proj(x2, g, w):
    n = x2.shape[0]
    row = lambda i: (i, 0)
    fixed = lambda i: (0, 0)
    return pl.pallas_call(
        _in_proj_kernel,
        grid=(n // TM_PROJ,),
        in_specs=[pl.BlockSpec((TM_PROJ, D_MODEL), row),
                  pl.BlockSpec((1, D_MODEL), fixed),
                  pl.BlockSpec((D_MODEL, D_IN), fixed, pipeline_mode=pl.Buffered(1))],
        out_specs=[pl.BlockSpec((TM_PROJ, D_ATTN), row), pl.BlockSpec((TM_PROJ, 2 * D_ATTN), row),
                   pl.BlockSpec((TM_PROJ, 2 * D_ATTN), row), pl.BlockSpec((TM_PROJ, D_SSM), row)],
        out_shape=[jax.ShapeDtypeStruct((n, D_ATTN), BF16), jax.ShapeDtypeStruct((n, 2 * D_ATTN), BF16),
                   jax.ShapeDtypeStruct((n, 2 * D_ATTN), BF16), jax.ShapeDtypeStruct((n, D_SSM), F32)],
        scratch_shapes=[pltpu.VMEM((D_MODEL, D_IN), BF16)],
        compiler_params=pltpu.CompilerParams(dimension_semantics=("arbitrary",),
                                             vmem_limit_bytes=VMEM_LIMIT_BYTES),
        name="in_proj",
    )(x2, g, w)


def _attn_kernel(q_ref, k_ref, v_ref, g_ref, cmat_ref, o_ref, z_scr, w_scr, r_scr, acc_scr):
    step = pl.program_id(1)
    row = lax.broadcasted_iota(jnp.int32, (BLK, BLK), 0)
    col = lax.broadcasted_iota(jnp.int32, (BLK, BLK), 1)
    before = col < row
    cmat = cmat_ref[...]
    nt_dims = (((1,), (1,)), ((), ()))
    first = [step * Q_CHAINS + c for c in range(Q_CHAINS)]

    def key_rows(j, c):
        return pl.ds(pl.multiple_of(jnp.clip(j, 0, first[c]) * BLK, BLK), BLK)

    def per_head_rows(ref, j, c, p):
        rows = key_rows(j, c)
        return jnp.concatenate([ref[rows, (2 * p) * LANES:(2 * p + 1) * LANES],
                                ref[rows, (2 * p + 1) * LANES:(2 * p + 2) * LANES]], axis=0)

    def scores(j, c, p):
        qp = q_ref[c * BLK:(c + 1) * BLK, p * LANES:(p + 1) * LANES]
        return lax.dot_general(qp, per_head_rows(k_ref, j, c, p), nt_dims,
                               preferred_element_type=F32)

    def apply_weights(j, c, p):
        u = c * HEAD_PAIRS + p
        acc_scr[u] += jnp.dot(w_scr[u], per_head_rows(v_ref, j, c, p), preferred_element_type=F32)

    for c in range(Q_CHAINS):
        for p in range(HEAD_PAIRS):
            z = scores(first[c], c, p)
            for hh in range(2):
                keys = slice(hh * BLK, (hh + 1) * BLK)
                z_scr[c * HEAD_PAIRS + p, :, keys] = jnp.where(before, z[:, keys], -MASKED)
    for u in range(Q_CHAINS * HEAD_PAIRS):
        acc_scr[u] = jnp.zeros((BLK, LANES), F32)

    def sweep(t, diagonal):
        m = None
        for c in range(Q_CHAINS):
            sps = []
            for h in range(ATTN_HEADS):
                z = z_scr[c * HEAD_PAIRS + h // 2, :, (h % 2) * BLK:(h % 2 + 1) * BLK]
                sp = jnp.maximum(z, 0.0) + jnp.log(1.0 + jnp.exp2(jnp.abs(z) * -LOG2_E))
                sps.append(sp.astype(BF16))
            cs = jnp.dot(jnp.concatenate(sps, axis=0), cmat,
                         preferred_element_type=F32)
            if not diagonal:
                for p in range(HEAD_PAIRS):
                    apply_weights(first[c] - t + 1, c, p)
            worst = None
            for h in range(ATTN_HEADS):
                n = c * ATTN_HEADS + h
                csn = cs[h * BLK:(h + 1) * BLK]
                z = z_scr[c * HEAD_PAIRS + h // 2, :, (h % 2) * BLK:(h % 2 + 1) * BLK]
                if diagonal:
                    r_new = csn[:, BLK:]
                    log_w = z + csn[:, :BLK]
                else:
                    r = r_scr[n]
                    r_new = r + csn[:, BLK:]
                    log_w = z + csn[:, :BLK] + r
                r_scr[n] = r_new
                worst = r_new if worst is None else jnp.maximum(worst, r_new)
                w_scr[c * HEAD_PAIRS + h // 2, :, (h % 2) * BLK:(h % 2 + 1) * BLK] = (
                    jnp.exp2(log_w * LOG2_E).astype(BF16))
            for p in range(HEAD_PAIRS):
                z_scr[c * HEAD_PAIRS + p] = scores(first[c] - t - 1, c, p)
            mc = jnp.where(first[c] - t - 1 >= 0, jnp.max(worst), -jnp.inf)
            m = mc if m is None else jnp.maximum(m, mc)
        for c in range(Q_CHAINS):
            @pl.when(first[c] - t - 1 < 0)
            def _(c=c):
                for h in range(ATTN_HEADS):
                    r_scr[c * ATTN_HEADS + h] = jnp.full((BLK, BLK), -MASKED, F32)
        return m

    def body(carry):
        t, _ = carry
        return t + 1, sweep(t, False)

    def cond(carry):
        return carry[1] > LOG_W_ZERO

    t_end, _ = lax.while_loop(cond, body, (jnp.int32(1), sweep(0, True)))
    for c in range(Q_CHAINS):
        for p in range(HEAD_PAIRS):
            apply_weights(first[c] - t_end + 1, c, p)

    for c in range(Q_CHAINS):
        o = jnp.concatenate([acc_scr[c * HEAD_PAIRS + p] for p in range(HEAD_PAIRS)], axis=1)
        o_ref[c * BLK:(c + 1) * BLK, :] = (_rms(o) * g_ref[...]).astype(BF16)


def _attention(q, k, v, g, cmat, batch, seq):
    n = q.shape[0]
    rows = Q_CHAINS * BLK
    steps = seq // rows
    qrow = lambda b, i: (b * steps + i, 0)
    kv = lambda b, i: (b, 0)
    fixed = lambda b, i: (0, 0)
    pairs = Q_CHAINS * HEAD_PAIRS
    return pl.pallas_call(
        _attn_kernel,
        grid=(batch, steps),
        in_specs=[pl.BlockSpec((rows, D_ATTN), qrow),
                  pl.BlockSpec((seq, 2 * D_ATTN), kv),
                  pl.BlockSpec((seq, 2 * D_ATTN), kv),
                  pl.BlockSpec((1, D_ATTN), fixed),
                  pl.BlockSpec((BLK, 2 * BLK), fixed)],
        out_specs=pl.BlockSpec((rows, D_ATTN), qrow),
        out_shape=jax.ShapeDtypeStruct((n, D_ATTN), BF16),
        scratch_shapes=[pltpu.VMEM((pairs, BLK, 2 * BLK), F32),
                        pltpu.VMEM((pairs, BLK, 2 * BLK), BF16),
                        pltpu.VMEM((Q_CHAINS * ATTN_HEADS, BLK, BLK), F32),
                        pltpu.VMEM((pairs, BLK, LANES), F32)],
        compiler_params=pltpu.CompilerParams(dimension_semantics=("arbitrary", "arbitrary"),
                                             vmem_limit_bytes=VMEM_LIMIT_BYTES),
        name="attention",
    )(q, k, v, g, cmat)


def _ssm_kernel(u_ref, bre_ref, bim_ref, lbre_ref, lbim_ref, cre_ref, cim_ref, d_ref,
                y_ref, utb_scr, ytb_scr, sr_scr, si_scr, *x_scr):
    xr_scr, xi_scr = x_scr[:SCAN_CHUNKS], x_scr[SCAN_CHUNKS:]

    @pl.when(pl.program_id(0) == 0)
    def _():
        sr_scr[...] = jnp.zeros_like(sr_scr)
        si_scr[...] = jnp.zeros_like(si_scr)

    for b in range(SUBLANES):
        for s in range(SSM_SLABS):
            utb_scr[s, pl.ds(b, SCAN_T, stride=SUBLANES), :] = u_ref[b, :, s * LANES:(s + 1) * LANES]

    slabs_per_slice = MXU_DIM // LANES
    chunks_per_slice = SCAN_CHUNKS // K_SLICES
    uk = [jnp.concatenate([utb_scr[k * slabs_per_slice + s] for s in range(slabs_per_slice)],
                          axis=1).astype(BF16) for k in range(K_SLICES)]

    def project_in(c):
        k, part = divmod(c, chunks_per_slice)
        cols = slice(part * SCAN_COLS, (part + 1) * SCAN_COLS)
        xr_scr[c][...] = jnp.dot(uk[k], bre_ref[k, :, cols], preferred_element_type=F32)
        xi_scr[c][...] = jnp.dot(uk[k], bim_ref[k, :, cols], preferred_element_type=F32)

    def recur(c):
        cols = slice(c * SCAN_COLS, (c + 1) * SCAN_COLS)
        lr = jnp.broadcast_to(lbre_ref[:, cols], (SUBLANES, SCAN_COLS))
        li = jnp.broadcast_to(lbim_ref[:, cols], (SUBLANES, SCAN_COLS))
        sr, si = sr_scr[:, cols], si_scr[:, cols]
        for t in range(SCAN_T):
            rows = slice(t * SUBLANES, (t + 1) * SUBLANES)
            sr, si = (lr * sr - li * si + xr_scr[c][rows, :],
                      lr * si + li * sr + xi_scr[c][rows, :])
            xr_scr[c][rows, :] = sr
            xi_scr[c][rows, :] = si
        sr_scr[:, cols] = sr
        si_scr[:, cols] = si

    def project_out(c):
        k, part = divmod(c, chunks_per_slice)
        rows = slice(part * SCAN_COLS, (part + 1) * SCAN_COLS)
        y = jnp.dot(xr_scr[c][...].astype(BF16), cre_ref[k, rows, :], preferred_element_type=F32)
        y = y - jnp.dot(xi_scr[c][...].astype(BF16), cim_ref[k, rows, :], preferred_element_type=F32)
        for s in range(slabs_per_slice):
            slab = k * slabs_per_slice + s
            lanes = slice(slab * LANES, (slab + 1) * LANES)
            ys = y[:, s * LANES:(s + 1) * LANES]
            if part == 0:
                ytb_scr[slab] = ys + d_ref[:, lanes] * utb_scr[slab]
            else:
                ytb_scr[slab] += ys

    for stage in range(SCAN_CHUNKS + 2):
        if stage < SCAN_CHUNKS:
            project_in(stage)
        if 1 <= stage <= SCAN_CHUNKS:
            recur(stage - 1)
        if stage >= 2:
            project_out(stage - 2)

    for b in range(SUBLANES):
        for s in range(SSM_SLABS):
            y_ref[b, :, s * LANES:(s + 1) * LANES] = ytb_scr[s, pl.ds(b, SCAN_T, stride=SUBLANES), :]


def _ssm_scan(u3, bre, bim, lbre, lbim, cre, cim, d):
    batch, seq, _ = u3.shape
    rows = SCAN_T * batch
    chunk = lambda i: (0, i, 0)
    fixed2 = lambda i: (0, 0)
    fixed3 = lambda i: (0, 0, 0)
    return pl.pallas_call(
        _ssm_kernel,
        grid=(seq // SCAN_T,),
        in_specs=[pl.BlockSpec((batch, SCAN_T, D_SSM), chunk),
                  pl.BlockSpec((K_SLICES, MXU_DIM, STATE_PER_SLICE), fixed3),
                  pl.BlockSpec((K_SLICES, MXU_DIM, STATE_PER_SLICE), fixed3),
                  pl.BlockSpec((1, N_STATE), fixed2),
                  pl.BlockSpec((1, N_STATE), fixed2),
                  pl.BlockSpec((K_SLICES, STATE_PER_SLICE, MXU_DIM), fixed3),
                  pl.BlockSpec((K_SLICES, STATE_PER_SLICE, MXU_DIM), fixed3),
                  pl.BlockSpec((1, D_SSM), fixed2)],
        out_specs=pl.BlockSpec((batch, SCAN_T, D_SSM), chunk),
        out_shape=jax.ShapeDtypeStruct((batch, seq, D_SSM), F32),
        scratch_shapes=[pltpu.VMEM((SSM_SLABS, rows, LANES), F32), pltpu.VMEM((SSM_SLABS, rows, LANES), F32),
                        pltpu.VMEM((SUBLANES, N_STATE), F32), pltpu.VMEM((SUBLANES, N_STATE), F32)]
        + [pltpu.VMEM((rows, SCAN_COLS), F32)] * (2 * SCAN_CHUNKS),
        compiler_params=pltpu.CompilerParams(dimension_semantics=("arbitrary",),
                                             vmem_limit_bytes=VMEM_LIMIT_BYTES),
        name="ssm_scan",
    )(u3, bre, bim, lbre, lbim, cre, cim, d)


def _post_kernel(x_ref, oa_ref, y_ref, wglu_ref, gs_ref, wout_ref, g2_ref,
                 wg_ref, wu_ref, wd_ref, gf_ref, o_ref):
    for s in range(TM_POST // SUB_ROWS):
        rows = slice(s * SUB_ROWS, (s + 1) * SUB_ROWS)
        yg = jax.nn.gelu(y_ref[rows, :])
        gate = jnp.dot(yg.astype(BF16), wglu_ref[...], preferred_element_type=F32)
        y2 = yg * jax.nn.sigmoid(gate)
        o_ssm = (_rms(y2) * gs_ref[...]).astype(BF16)
        x1 = x_ref[rows, :] + jnp.dot(oa_ref[rows, :], wout_ref[0:D_ATTN, :], preferred_element_type=F32)
        x1 = x1 + jnp.dot(o_ssm, wout_ref[D_ATTN:, :], preferred_element_type=F32)
        h2 = (_rms(x1) * g2_ref[...]).astype(BF16)
        o_ref[rows, :] = x1
        for c in range(D_FF // FF_CHUNK):
            cols = slice(c * FF_CHUNK, (c + 1) * FF_CHUNK)
            gt = jnp.dot(h2, wg_ref[:, cols], preferred_element_type=F32)
            up = jnp.dot(h2, wu_ref[:, cols], preferred_element_type=F32)
            a = (gt * jax.nn.sigmoid(gt) * up).astype(BF16)
            o_ref[rows, :] += jnp.dot(a, wd_ref[cols, :], preferred_element_type=F32)
        o_ref[rows, :] = _rms(o_ref[rows, :]) * gf_ref[...]


def _post(x2, oa, y, wglu, gs, wout, g2, wg, wu, wd, gf):
    n = x2.shape[0]
    row = lambda i: (i, 0)
    fixed = lambda i: (0, 0)
    once = functools.partial(pl.BlockSpec, index_map=fixed, pipeline_mode=pl.Buffered(1))
    return pl.pallas_call(
        _post_kernel,
        grid=(n // TM_POST,),
        in_specs=[pl.BlockSpec((TM_POST, D_MODEL), row),
                  pl.BlockSpec((TM_POST, D_ATTN), row),
                  pl.BlockSpec((TM_POST, D_SSM), row),
                  once((D_SSM, D_SSM)),
                  once((1, D_SSM)),
                  once((D_MODEL, D_MODEL)),
                  once((1, D_MODEL)),
                  once((D_MODEL, D_FF)),
                  once((D_MODEL, D_FF)),
                  once((D_FF, D_MODEL)),
                  once((1, D_MODEL))],
        out_specs=pl.BlockSpec((TM_POST, D_MODEL), row),
        out_shape=jax.ShapeDtypeStruct((n, D_MODEL), F32),
        compiler_params=pltpu.CompilerParams(dimension_semantics=("arbitrary",),
                                             vmem_limit_bytes=VMEM_LIMIT_BYTES),
        name="post",
    )(x2, oa, y, wglu, gs, wout, g2, wg, wu, wd, gf)


def _block_diag_b(bb):
    gl = MXU_DIM // SSM_GROUP
    eye = jnp.eye(gl, dtype=bb.dtype)
    t = bb.reshape(K_SLICES, gl, SSM_GROUP, SSM_STATE)
    m = t[:, :, :, None, :] * eye[None, :, None, :, None]
    return m.reshape(K_SLICES, MXU_DIM, STATE_PER_SLICE).astype(BF16)


def _block_diag_c(c):
    gl = MXU_DIM // SSM_GROUP
    eye = jnp.eye(gl, dtype=c.dtype)
    t = c.reshape(K_SLICES, gl, SSM_GROUP, SSM_STATE).transpose(0, 1, 3, 2)
    m = t[:, :, :, None, :] * eye[None, :, None, :, None]
    return m.reshape(K_SLICES, STATE_PER_SLICE, MXU_DIM).astype(BF16)


def _cumsum_matrix():
    j = jnp.arange(BLK)[:, None]
    s = jnp.arange(BLK)[None, :]
    tri = -(j >= s).astype(BF16)
    return jnp.concatenate([tri, -jnp.ones((BLK, BLK), BF16)], axis=1)


def kernel(x, norm1_g, w_in, attn_norm_g, lambda_re, lambda_im, log_step, b_re, b_im, c_re, c_im,
           d_skip, w_glu, ssm_norm_g, w_out, norm2_g, w_gate, w_up, w_down, final_norm_g):
    batch, seq, d_model = x.shape
    assert d_model == D_MODEL and norm1_g.shape[0] == 1 and batch == SUBLANES
    n = batch * seq
    x2 = x.reshape(n, D_MODEL)

    q, k, v, u = _in_proj(x2, norm1_g[0][None], w_in[0])
    oa = _attention(q, k, v, attn_norm_g[0][None], _cumsum_matrix(), batch, seq)

    lb_re, lb_im, bb_re, bb_im = _ssm_prep(
        lambda_re[0], lambda_im[0], log_step[0][:, None],
        b_re[0].transpose(0, 2, 1), b_im[0].transpose(0, 2, 1))
    y = _ssm_scan(u.reshape(batch, seq, D_SSM), _block_diag_b(bb_re), _block_diag_b(bb_im),
                  lb_re.reshape(1, N_STATE), lb_im.reshape(1, N_STATE),
                  _block_diag_c(c_re[0]), _block_diag_c(c_im[0]), d_skip[0].reshape(1, D_SSM))

    out = _post(x2, oa, y.reshape(n, D_SSM), w_glu[0].astype(BF16), ssm_norm_g[0][None],
                w_out[0].astype(BF16), norm2_g[0][None], w_gate[0].astype(BF16),
                w_up[0].astype(BF16), w_down[0].astype(BF16), final_norm_g[None])
    return out.reshape(batch, seq, D_MODEL)
```

```python
import functools

import jax
import jax.numpy as jnp
from jax import lax
from jax.experimental import pallas as pl
from jax.experimental.pallas import tpu as pltpu

F32 = jnp.float32
BF16 = jnp.bfloat16

D_MODEL = 1024
ATTN_HEADS = 8
HEAD_DIM = 64
D_ATTN = ATTN_HEADS * HEAD_DIM
D_SSM = D_MODEL - D_ATTN
SSM_GROUP = 16
SSM_GROUPS = D_SSM // SSM_GROUP
SSM_STATE = 64
N_STATE = SSM_GROUPS * SSM_STATE
D_FF = 2816
D_IN = 3 * D_ATTN + D_SSM
EPS = 1e-6

LANES = 128
SUBLANES = 8
MXU_DIM = 256
VMEM_LIMIT_BYTES = 56 * 1024 * 1024

SUB_ROWS = 512
TM_PROJ = 2 * SUB_ROWS
TM_POST = 2 * SUB_ROWS
FF_CHUNK = 256
BLK = 128
HEAD_PAIRS = ATTN_HEADS // 2
Q_CHAINS = 8
SCAN_T = 128
SCAN_COLS = 512
SCAN_CHUNKS = N_STATE // SCAN_COLS
K_SLICES = D_SSM // MXU_DIM
STATE_PER_SLICE = N_STATE // K_SLICES
SSM_SLABS = D_SSM // LANES
LOG_W_ZERO = -104.0
MASKED = 1e30
LOG2_E = 1.4426950408889634


def _rms(x):
    return x * lax.rsqrt(jnp.mean(x * x, axis=-1, keepdims=True) + EPS)


def _ssm_prep_kernel(lre_ref, lim_ref, ls_ref, bre_ref, bim_ref,
                     lbre_ref, lbim_ref, bbre_ref, bbim_ref):
    lam_re = lre_ref[...]
    lam_im = lim_ref[...]
    dt = jnp.exp(ls_ref[...])
    mag = jnp.exp(lam_re * dt)
    ang = lam_im * dt
    lb_re = mag * jnp.cos(ang)
    lb_im = mag * jnp.sin(ang)
    den = lam_re * lam_re + lam_im * lam_im
    num_re = lb_re - 1.0
    f_re = (num_re * lam_re + lb_im * lam_im) / den
    f_im = (lb_im * lam_re - num_re * lam_im) / den
    lbre_ref[...] = lb_re
    lbim_ref[...] = lb_im
    br = bre_ref[...]
    bi = bim_ref[...]
    bbre_ref[...] = f_re[:, None, :] * br - f_im[:, None, :] * bi
    bbim_ref[...] = f_re[:, None, :] * bi + f_im[:, None, :] * br


def _ssm_prep(lambda_re, lambda_im, log_step, b_re_t, b_im_t):
    g, p = lambda_re.shape
    h = b_re_t.shape[1]
    return pl.pallas_call(
        _ssm_prep_kernel,
        out_shape=(jax.ShapeDtypeStruct((g, p), F32), jax.ShapeDtypeStruct((g, p), F32),
                   jax.ShapeDtypeStruct((g, h, p), F32), jax.ShapeDtypeStruct((g, h, p), F32)),
        name="ssm_prep",
    )(lambda_re, lambda_im, log_step, b_re_t, b_im_t)


def _in_proj_kernel(x_ref, g_ref, w32_ref, q_ref, k_ref, v_ref, u_ref, w_ref):
    @pl.when(pl.program_id(0) == 0)
    def _():
        w_ref[...] = w32_ref[...].astype(BF16)

    scale = HEAD_DIM ** -0.5
    low_half = lax.broadcasted_iota(jnp.int32, (SUB_ROWS, LANES), 1) < HEAD_DIM
    for s in range(TM_PROJ // SUB_ROWS):
        rows = slice(s * SUB_ROWS, (s + 1) * SUB_ROWS)
        packed_rows = slice(s * SUB_ROWS // 2, (s + 1) * SUB_ROWS // 2)
        h = (_rms(x_ref[rows, :]) * g_ref[...]).astype(BF16)
        q = jnp.dot(h, w_ref[:, 0:D_ATTN], preferred_element_type=F32)
        q_ref[rows, :] = (q * scale).astype(BF16)
        k = jnp.dot(h, w_ref[:, D_ATTN:2 * D_ATTN], preferred_element_type=F32)
        v = jnp.dot(h, w_ref[:, 2 * D_ATTN:3 * D_ATTN], preferred_element_type=F32)
        for full, ref in ((k, k_ref), (v, v_ref)):
            for p in range(HEAD_PAIRS):
                pair = full[:, p * LANES:(p + 1) * LANES].astype(BF16)
                zero = jnp.zeros_like(pair)
                ref[packed_rows, (2 * p) * LANES:(2 * p + 1) * LANES] = pltpu.bitcast(
                    jnp.where(low_half, pair, zero), jnp.uint32)
                ref[packed_rows, (2 * p + 1) * LANES:(2 * p + 2) * LANES] = pltpu.bitcast(
                    jnp.where(low_half, zero, pair), jnp.uint32)
        u_ref[rows, :] = jnp.dot(h, w_ref[:, 3 * D_ATTN:], preferred_element_type=F32)


def _in_proj(x2, g, w):
    n = x2.shape[0]
    row = lambda i: (i, 0)
    fixed = lambda i: (0, 0)
    return pl.pallas_call(
        _in_proj_kernel,
        grid=(n // TM_PROJ,),
        in_specs=[pl.BlockSpec((TM_PROJ, D_MODEL), row),
                  pl.BlockSpec((1, D_MODEL), fixed),
                  pl.BlockSpec((D_MODEL, D_IN), fixed, pipeline_mode=pl.Buffered(1))],
        out_specs=[pl.BlockSpec((TM_PROJ, D_ATTN), row), pl.BlockSpec((TM_PROJ // 2, 2 * D_ATTN), row),
                   pl.BlockSpec((TM_PROJ // 2, 2 * D_ATTN), row), pl.BlockSpec((TM_PROJ, D_SSM), row)],
        out_shape=[jax.ShapeDtypeStruct((n, D_ATTN), BF16),
                   jax.ShapeDtypeStruct((n // 2, 2 * D_ATTN), jnp.uint32),
                   jax.ShapeDtypeStruct((n // 2, 2 * D_ATTN), jnp.uint32),
                   jax.ShapeDtypeStruct((n, D_SSM), F32)],
        scratch_shapes=[pltpu.VMEM((D_MODEL, D_IN), BF16)],
        compiler_params=pltpu.CompilerParams(dimension_semantics=("arbitrary",),
                                             vmem_limit_bytes=VMEM_LIMIT_BYTES),
        name="in_proj",
    )(x2, g, w)


def _attn_kernel(q_ref, k_ref, v_ref, g_ref, cmat_ref, o_ref, z_scr, w_scr, r_scr, acc_scr):
    step = pl.program_id(1)
    row = lax.broadcasted_iota(jnp.int32, (BLK, BLK), 0)
    col = lax.broadcasted_iota(jnp.int32, (BLK, BLK), 1)
    before = col < row
    cmat = cmat_ref[...]
    nt_dims = (((1,), (1,)), ((), ()))
    first = [step * Q_CHAINS + c for c in range(Q_CHAINS)]

    def key_rows(j, c):
        packed = BLK // 2
        return pl.ds(pl.multiple_of(jnp.clip(j, 0, first[c]) * packed, packed), packed)

    def per_head_rows(ref, j, c, p):
        rows = key_rows(j, c)
        return jnp.concatenate(
            [pltpu.bitcast(ref[rows, (2 * p) * LANES:(2 * p + 1) * LANES], BF16),
             pltpu.bitcast(ref[rows, (2 * p + 1) * LANES:(2 * p + 2) * LANES], BF16)], axis=0)

    def scores(j, c, p):
        qp = q_ref[c * BLK:(c + 1) * BLK, p * LANES:(p + 1) * LANES]
        return lax.dot_general(qp, per_head_rows(k_ref, j, c, p), nt_dims,
                               preferred_element_type=F32)

    def apply_weights(j, c, p):
        u = c * HEAD_PAIRS + p
        acc_scr[u] += jnp.dot(w_scr[u], per_head_rows(v_ref, j, c, p), preferred_element_type=F32)

    for c in range(Q_CHAINS):
        for p in range(HEAD_PAIRS):
            z = scores(first[c], c, p)
            for hh in range(2):
                keys = slice(hh * BLK, (hh + 1) * BLK)
                z_scr[c * HEAD_PAIRS + p, :, keys] = jnp.where(before, z[:, keys], -MASKED)
    for u in range(Q_CHAINS * HEAD_PAIRS):
        acc_scr[u] = jnp.zeros((BLK, LANES), F32)

    def sweep(t, diagonal):
        m = None
        for c in range(Q_CHAINS):
            sps = []
            for h in range(ATTN_HEADS):
                z = z_scr[c * HEAD_PAIRS + h // 2, :, (h % 2) * BLK:(h % 2 + 1) * BLK]
                sp = jnp.maximum(z, 0.0) + jnp.log(1.0 + jnp.exp2(jnp.abs(z) * -LOG2_E))
                sps.append(sp.astype(BF16))
            cs = jnp.dot(jnp.concatenate(sps, axis=0), cmat,
                         preferred_element_type=F32)
            if not diagonal:
                for p in range(HEAD_PAIRS):
                    apply_weights(first[c] - t + 1, c, p)
            worst = None
            for h in range(ATTN_HEADS):
                n = c * ATTN_HEADS + h
                csn = cs[h * BLK:(h + 1) * BLK]
                z = z_scr[c * HEAD_PAIRS + h // 2, :, (h % 2) * BLK:(h % 2 + 1) * BLK]
                if diagonal:
                    r_new = csn[:, BLK:]
                    log_w = z + csn[:, :BLK]
                else:
                    r = r_scr[n]
                    r_new = r + csn[:, BLK:]
                    log_w = z + csn[:, :BLK] + r
                r_scr[n] = r_new
                worst = r_new if worst is None else jnp.maximum(worst, r_new)
                w_scr[c * HEAD_PAIRS + h // 2, :, (h % 2) * BLK:(h % 2 + 1) * BLK] = (
                    jnp.exp2(log_w * LOG2_E).astype(BF16))
            for p in range(HEAD_PAIRS):
                z_scr[c * HEAD_PAIRS + p] = scores(first[c] - t - 1, c, p)
            mc = jnp.where(first[c] - t - 1 >= 0, jnp.max(worst), -jnp.inf)
            m = mc if m is None else jnp.maximum(m, mc)
        for c in range(Q_CHAINS):
            @pl.when(first[c] - t - 1 < 0)
            def _(c=c):
                for h in range(ATTN_HEADS):
                    r_scr[c * ATTN_HEADS + h] = jnp.full((BLK, BLK), -MASKED, F32)
        return m

    def body(carry):
        t, _ = carry
        return t + 1, sweep(t, False)

    def cond(carry):
        return carry[1] > LOG_W_ZERO

    t_end, _ = lax.while_loop(cond, body, (jnp.int32(1), sweep(0, True)))
    for c in range(Q_CHAINS):
        for p in range(HEAD_PAIRS):
            apply_weights(first[c] - t_end + 1, c, p)

    for c in range(Q_CHAINS):
        o = jnp.concatenate([acc_scr[c * HEAD_PAIRS + p] for p in range(HEAD_PAIRS)], axis=1)
        o_ref[c * BLK:(c + 1) * BLK, :] = (_rms(o) * g_ref[...]).astype(BF16)


def _attention(q, k, v, g, cmat, batch, seq):
    n = q.shape[0]
    rows = Q_CHAINS * BLK
    steps = seq // rows
    qrow = lambda b, i: (b * steps + i, 0)
    kv = lambda b, i: (b, 0)
    fixed = lambda b, i: (0, 0)
    pairs = Q_CHAINS * HEAD_PAIRS
    return pl.pallas_call(
        _attn_kernel,
        grid=(batch, steps),
        in_specs=[pl.BlockSpec((rows, D_ATTN), qrow),
                  pl.BlockSpec((seq // 2, 2 * D_ATTN), kv),
                  pl.BlockSpec((seq // 2, 2 * D_ATTN), kv),
                  pl.BlockSpec((1, D_ATTN), fixed),
                  pl.BlockSpec((BLK, 2 * BLK), fixed)],
        out_specs=pl.BlockSpec((rows, D_ATTN), qrow),
        out_shape=jax.ShapeDtypeStruct((n, D_ATTN), BF16),
        scratch_shapes=[pltpu.VMEM((pairs, BLK, 2 * BLK), F32),
                        pltpu.VMEM((pairs, BLK, 2 * BLK), BF16),
                        pltpu.VMEM((Q_CHAINS * ATTN_HEADS, BLK, BLK), F32),
                        pltpu.VMEM((pairs, BLK, LANES), F32)],
        compiler_params=pltpu.CompilerParams(dimension_semantics=("arbitrary", "arbitrary"),
                                             vmem_limit_bytes=VMEM_LIMIT_BYTES),
        name="attention",
    )(q, k, v, g, cmat)


def _ssm_kernel(u_ref, bre_ref, bim_ref, lbre_ref, lbim_ref, cre_ref, cim_ref, d_ref,
                y_ref, utb_scr, ytb_scr, sr_scr, si_scr, *x_scr):
    xr_scr, xi_scr = x_scr[:SCAN_CHUNKS], x_scr[SCAN_CHUNKS:]

    @pl.when(pl.program_id(0) == 0)
    def _():
        sr_scr[...] = jnp.zeros_like(sr_scr)
        si_scr[...] = jnp.zeros_like(si_scr)

    for b in range(SUBLANES):
        for s in range(SSM_SLABS):
            utb_scr[s, pl.ds(b, SCAN_T, stride=SUBLANES), :] = u_ref[b, :, s * LANES:(s + 1) * LANES]

    slabs_per_slice = MXU_DIM // LANES
    chunks_per_slice = SCAN_CHUNKS // K_SLICES
    uk = [jnp.concatenate([utb_scr[k * slabs_per_slice + s] for s in range(slabs_per_slice)],
                          axis=1).astype(BF16) for k in range(K_SLICES)]

    def project_in(c):
        k, part = divmod(c, chunks_per_slice)
        cols = slice(part * SCAN_COLS, (part + 1) * SCAN_COLS)
        xr_scr[c][...] = jnp.dot(uk[k], bre_ref[k, :, cols], preferred_element_type=F32)
        xi_scr[c][...] = jnp.dot(uk[k], bim_ref[k, :, cols], preferred_element_type=F32)

    def recur(c):
        cols = slice(c * SCAN_COLS, (c + 1) * SCAN_COLS)
        lr = jnp.broadcast_to(lbre_ref[:, cols], (SUBLANES, SCAN_COLS))
        li = jnp.broadcast_to(lbim_ref[:, cols], (SUBLANES, SCAN_COLS))
        sr, si = sr_scr[:, cols], si_scr[:, cols]
        for t in range(SCAN_T):
            rows = slice(t * SUBLANES, (t + 1) * SUBLANES)
            sr, si = (lr * sr - li * si + xr_scr[c][rows, :],
                      lr * si + li * sr + xi_scr[c][rows, :])
            xr_scr[c][rows, :] = sr
            xi_scr[c][rows, :] = si
        sr_scr[:, cols] = sr
        si_scr[:, cols] = si

    def project_out(c):
        k, part = divmod(c, chunks_per_slice)
        rows = slice(part * SCAN_COLS, (part + 1) * SCAN_COLS)
        y = jnp.dot(xr_scr[c][...].astype(BF16), cre_ref[k, rows, :], preferred_element_type=F32)
        y = y - jnp.dot(xi_scr[c][...].astype(BF16), cim_ref[k, rows, :], preferred_element_type=F32)
        for s in range(slabs_per_slice):
            slab = k * slabs_per_slice + s
            lanes = slice(slab * LANES, (slab + 1) * LANES)
            ys = y[:, s * LANES:(s + 1) * LANES]
            if part == 0:
                ytb_scr[slab] = ys + d_ref[:, lanes] * utb_scr[slab]
            else:
                ytb_scr[slab] += ys

    for stage in range(SCAN_CHUNKS + 2):
        if stage < SCAN_CHUNKS:
            project_in(stage)
        if 1 <= stage <= SCAN_CHUNKS:
            recur(stage - 1)
        if stage >= 2:
            project_out(stage - 2)

    for b in range(SUBLANES):
        for s in range(SSM_SLABS):
            y_ref[b, :, s * LANES:(s + 1) * LANES] = ytb_scr[s, pl.ds(b, SCAN_T, stride=SUBLANES), :]


def _ssm_scan(u3, bre, bim, lbre, lbim, cre, cim, d):
    batch, seq, _ = u3.shape
    rows = SCAN_T * batch
    chunk = lambda i: (0, i, 0)
    fixed2 = lambda i: (0, 0)
    fixed3 = lambda i: (0, 0, 0)
    return pl.pallas_call(
        _ssm_kernel,
        grid=(seq // SCAN_T,),
        in_specs=[pl.BlockSpec((batch, SCAN_T, D_SSM), chunk),
                  pl.BlockSpec((K_SLICES, MXU_DIM, STATE_PER_SLICE), fixed3),
                  pl.BlockSpec((K_SLICES, MXU_DIM, STATE_PER_SLICE), fixed3),
                  pl.BlockSpec((1, N_STATE), fixed2),
                  pl.BlockSpec((1, N_STATE), fixed2),
                  pl.BlockSpec((K_SLICES, STATE_PER_SLICE, MXU_DIM), fixed3),
                  pl.BlockSpec((K_SLICES, STATE_PER_SLICE, MXU_DIM), fixed3),
                  pl.BlockSpec((1, D_SSM), fixed2)],
        out_specs=pl.BlockSpec((batch, SCAN_T, D_SSM), chunk),
        out_shape=jax.ShapeDtypeStruct((batch, seq, D_SSM), F32),
        scratch_shapes=[pltpu.VMEM((SSM_SLABS, rows, LANES), F32), pltpu.VMEM((SSM_SLABS, rows, LANES), F32),
                        pltpu.VMEM((SUBLANES, N_STATE), F32), pltpu.VMEM((SUBLANES, N_STATE), F32)]
        + [pltpu.VMEM((rows, SCAN_COLS), F32)] * (2 * SCAN_CHUNKS),
        compiler_params=pltpu.CompilerParams(dimension_semantics=("arbitrary",),
                                             vmem_limit_bytes=VMEM_LIMIT_BYTES),
        name="ssm_scan",
    )(u3, bre, bim, lbre, lbim, cre, cim, d)


def _post_kernel(x_ref, oa_ref, y_ref, wglu_ref, gs_ref, wout_ref, g2_ref,
                 wg_ref, wu_ref, wd_ref, gf_ref, o_ref):
    for s in range(TM_POST // SUB_ROWS):
        rows = slice(s * SUB_ROWS, (s + 1) * SUB_ROWS)
        yg = jax.nn.gelu(y_ref[rows, :])
        gate = jnp.dot(yg.astype(BF16), wglu_ref[...], preferred_element_type=F32)
        y2 = yg * jax.nn.sigmoid(gate)
        o_ssm = (_rms(y2) * gs_ref[...]).astype(BF16)
        x1 = x_ref[rows, :] + jnp.dot(oa_ref[rows, :], wout_ref[0:D_ATTN, :], preferred_element_type=F32)
        x1 = x1 + jnp.dot(o_ssm, wout_ref[D_ATTN:, :], preferred_element_type=F32)
        h2 = (_rms(x1) * g2_ref[...]).astype(BF16)
        o_ref[rows, :] = x1
        for c in range(D_FF // FF_CHUNK):
            cols = slice(c * FF_CHUNK, (c + 1) * FF_CHUNK)
            gt = jnp.dot(h2, wg_ref[:, cols], preferred_element_type=F32)
            up = jnp.dot(h2, wu_ref[:, cols], preferred_element_type=F32)
            a = (gt * jax.nn.sigmoid(gt) * up).astype(BF16)
            o_ref[rows, :] += jnp.dot(a, wd_ref[cols, :], preferred_element_type=F32)
        o_ref[rows, :] = _rms(o_ref[rows, :]) * gf_ref[...]


def _post(x2, oa, y, wglu, gs, wout, g2, wg, wu, wd, gf):
    n = x2.shape[0]
    row = lambda i: (i, 0)
    fixed = lambda i: (0, 0)
    once = functools.partial(pl.BlockSpec, index_map=fixed, pipeline_mode=pl.Buffered(1))
    return pl.pallas_call(
        _post_kernel,
        grid=(n // TM_POST,),
        in_specs=[pl.BlockSpec((TM_POST, D_MODEL), row),
                  pl.BlockSpec((TM_POST, D_ATTN), row),
                  pl.BlockSpec((TM_POST, D_SSM), row),
                  once((D_SSM, D_SSM)),
                  once((1, D_SSM)),
                  once((D_MODEL, D_MODEL)),
                  once((1, D_MODEL)),
                  once((D_MODEL, D_FF)),
                  once((D_MODEL, D_FF)),
                  once((D_FF, D_MODEL)),
                  once((1, D_MODEL))],
        out_specs=pl.BlockSpec((TM_POST, D_MODEL), row),
        out_shape=jax.ShapeDtypeStruct((n, D_MODEL), F32),
        compiler_params=pltpu.CompilerParams(dimension_semantics=("arbitrary",),
                                             vmem_limit_bytes=VMEM_LIMIT_BYTES),
        name="post",
    )(x2, oa, y, wglu, gs, wout, g2, wg, wu, wd, gf)


def _block_diag_b(bb):
    gl = MXU_DIM // SSM_GROUP
    eye = jnp.eye(gl, dtype=bb.dtype)
    t = bb.reshape(K_SLICES, gl, SSM_GROUP, SSM_STATE)
    m = t[:, :, :, None, :] * eye[None, :, None, :, None]
    return m.reshape(K_SLICES, MXU_DIM, STATE_PER_SLICE).astype(BF16)


def _block_diag_c(c):
    gl = MXU_DIM // SSM_GROUP
    eye = jnp.eye(gl, dtype=c.dtype)
    t = c.reshape(K_SLICES, gl, SSM_GROUP, SSM_STATE).transpose(0, 1, 3, 2)
    m = t[:, :, :, None, :] * eye[None, :, None, :, None]
    return m.reshape(K_SLICES, STATE_PER_SLICE, MXU_DIM).astype(BF16)


def _cumsum_matrix():
    j = jnp.arange(BLK)[:, None]
    s = jnp.arange(BLK)[None, :]
    tri = -(j >= s).astype(BF16)
    return jnp.concatenate([tri, -jnp.ones((BLK, BLK), BF16)], axis=1)


def kernel(x, norm1_g, w_in, attn_norm_g, lambda_re, lambda_im, log_step, b_re, b_im, c_re, c_im,
           d_skip, w_glu, ssm_norm_g, w_out, norm2_g, w_gate, w_up, w_down, final_norm_g):
    batch, seq, d_model = x.shape
    assert d_model == D_MODEL and norm1_g.shape[0] == 1 and batch == SUBLANES
    n = batch * seq
    x2 = x.reshape(n, D_MODEL)

    q, k, v, u = _in_proj(x2, norm1_g[0][None], w_in[0])
    oa = _attention(q, k, v, attn_norm_g[0][None], _cumsum_matrix(), batch, seq)

    lb_re, lb_im, bb_re, bb_im = _ssm_prep(
        lambda_re[0], lambda_im[0], log_step[0][:, None],
        b_re[0].transpose(0, 2, 1), b_im[0].transpose(0, 2, 1))
    y = _ssm_scan(u.reshape(batch, seq, D_SSM), _block_diag_b(bb_re), _block_diag_b(bb_im),
                  lb_re.reshape(1, N_STATE), lb_im.reshape(1, N_STATE),
                  _block_diag_c(c_re[0]), _block_diag_c(c_im[0]), d_skip[0].reshape(1, D_SSM))

    out = _post(x2, oa, y.reshape(n, D_SSM), w_glu[0].astype(BF16), ssm_norm_g[0][None],
                w_out[0].astype(BF16), norm2_g[0][None], w_gate[0].astype(BF16),
                w_up[0].astype(BF16), w_down[0].astype(BF16), final_norm_g[None])
    return out.reshape(batch, seq, D_MODEL)
```

```python
import functools

import jax
import jax.numpy as jnp
from jax import lax
from jax.experimental import pallas as pl
from jax.experimental.pallas import tpu as pltpu

F32 = jnp.float32
BF16 = jnp.bfloat16

D_MODEL = 1024
ATTN_HEADS = 8
HEAD_DIM = 64
D_ATTN = ATTN_HEADS * HEAD_DIM
D_SSM = D_MODEL - D_ATTN
SSM_GROUP = 16
SSM_GROUPS = D_SSM // SSM_GROUP
SSM_STATE = 64
N_STATE = SSM_GROUPS * SSM_STATE
D_FF = 2816
D_IN = 3 * D_ATTN + D_SSM
EPS = 1e-6

LANES = 128
SUBLANES = 8
MXU_DIM = 256
VMEM_LIMIT_BYTES = 56 * 1024 * 1024

SUB_ROWS = 512
TM_PROJ = 2 * SUB_ROWS
TM_POST = 2 * SUB_ROWS
FF_CHUNK = 256
BLK = 128
HEAD_PAIRS = ATTN_HEADS // 2
Q_CHAINS = 8
SCAN_T = 128
SCAN_COLS = 256
SCAN_CHUNKS = N_STATE // SCAN_COLS
K_SLICES = D_SSM // MXU_DIM
STATE_PER_SLICE = N_STATE // K_SLICES
SSM_SLABS = D_SSM // LANES
LOG_W_ZERO = -104.0
MASKED = 1e30
LOG2_E = 1.4426950408889634


def _rms(x):
    return x * lax.rsqrt(jnp.mean(x * x, axis=-1, keepdims=True) + EPS)


def _ssm_prep_kernel(lre_ref, lim_ref, ls_ref, bre_ref, bim_ref,
                     lbre_ref, lbim_ref, bbre_ref, bbim_ref):
    lam_re = lre_ref[...]
    lam_im = lim_ref[...]
    dt = jnp.exp(ls_ref[...])
    mag = jnp.exp(lam_re * dt)
    ang = lam_im * dt
    lb_re = mag * jnp.cos(ang)
    lb_im = mag * jnp.sin(ang)
    den = lam_re * lam_re + lam_im * lam_im
    num_re = lb_re - 1.0
    f_re = (num_re * lam_re + lb_im * lam_im) / den
    f_im = (lb_im * lam_re - num_re * lam_im) / den
    lbre_ref[...] = lb_re
    lbim_ref[...] = lb_im
    br = bre_ref[...]
    bi = bim_ref[...]
    bbre_ref[...] = f_re[:, None, :] * br - f_im[:, None, :] * bi
    bbim_ref[...] = f_re[:, None, :] * bi + f_im[:, None, :] * br


def _ssm_prep(lambda_re, lambda_im, log_step, b_re_t, b_im_t):
    g, p = lambda_re.shape
    h = b_re_t.shape[1]
    return pl.pallas_call(
        _ssm_prep_kernel,
        out_shape=(jax.ShapeDtypeStruct((g, p), F32), jax.ShapeDtypeStruct((g, p), F32),
                   jax.ShapeDtypeStruct((g, h, p), F32), jax.ShapeDtypeStruct((g, h, p), F32)),
        name="ssm_prep",
    )(lambda_re, lambda_im, log_step, b_re_t, b_im_t)


def _in_proj_kernel(x_ref, g_ref, w32_ref, q_ref, k_ref, v_ref, u_ref, w_ref):
    @pl.when(pl.program_id(0) == 0)
    def _():
        w_ref[...] = w32_ref[...].astype(BF16)

    scale = HEAD_DIM ** -0.5
    low_half = lax.broadcasted_iota(jnp.int32, (SUB_ROWS, LANES), 1) < HEAD_DIM
    for s in range(TM_PROJ // SUB_ROWS):
        rows = slice(s * SUB_ROWS, (s + 1) * SUB_ROWS)
        packed_rows = slice(s * SUB_ROWS // 2, (s + 1) * SUB_ROWS // 2)
        h = (_rms(x_ref[rows, :]) * g_ref[...]).astype(BF16)
        q = jnp.dot(h, w_ref[:, 0:D_ATTN], preferred_element_type=F32)
        q_ref[rows, :] = (q * scale).astype(BF16)
        k = jnp.dot(h, w_ref[:, D_ATTN:2 * D_ATTN], preferred_element_type=F32)
        v = jnp.dot(h, w_ref[:, 2 * D_ATTN:3 * D_ATTN], preferred_element_type=F32)
        for full, ref in ((k, k_ref), (v, v_ref)):
            for p in range(HEAD_PAIRS):
                pair = full[:, p * LANES:(p + 1) * LANES].astype(BF16)
                zero = jnp.zeros_like(pair)
                ref[packed_rows, (2 * p) * LANES:(2 * p + 1) * LANES] = pltpu.bitcast(
                    jnp.where(low_half, pair, zero), jnp.uint32)
                ref[packed_rows, (2 * p + 1) * LANES:(2 * p + 2) * LANES] = pltpu.bitcast(
                    jnp.where(low_half, zero, pair), jnp.uint32)
        u_ref[rows, :] = jnp.dot(h, w_ref[:, 3 * D_ATTN:], preferred_element_type=F32)


def _in_proj(x2, g, w):
    n = x2.shape[0]
    row = lambda i: (i, 0)
    fixed = lambda i: (0, 0)
    return pl.pallas_call(
        _in_proj_kernel,
        grid=(n // TM_PROJ,),
        in_specs=[pl.BlockSpec((TM_PROJ, D_MODEL), row),
                  pl.BlockSpec((1, D_MODEL), fixed),
                  pl.BlockSpec((D_MODEL, D_IN), fixed, pipeline_mode=pl.Buffered(1))],
        out_specs=[pl.BlockSpec((TM_PROJ, D_ATTN), row), pl.BlockSpec((TM_PROJ // 2, 2 * D_ATTN), row),
                   pl.BlockSpec((TM_PROJ // 2, 2 * D_ATTN), row), pl.BlockSpec((TM_PROJ, D_SSM), row)],
        out_shape=[jax.ShapeDtypeStruct((n, D_ATTN), BF16),
                   jax.ShapeDtypeStruct((n // 2, 2 * D_ATTN), jnp.uint32),
                   jax.ShapeDtypeStruct((n // 2, 2 * D_ATTN), jnp.uint32),
                   jax.ShapeDtypeStruct((n, D_SSM), F32)],
        scratch_shapes=[pltpu.VMEM((D_MODEL, D_IN), BF16)],
        compiler_params=pltpu.CompilerParams(dimension_semantics=("arbitrary",),
                                             vmem_limit_bytes=VMEM_LIMIT_BYTES),
        name="in_proj",
    )(x2, g, w)


def _attn_kernel(q_ref, k_ref, v_ref, g_ref, cmat_ref, o_ref, z_scr, w_scr, r_scr, acc_scr):
    step = pl.program_id(1)
    row = lax.broadcasted_iota(jnp.int32, (BLK, BLK), 0)
    col = lax.broadcasted_iota(jnp.int32, (BLK, BLK), 1)
    before = col < row
    cmat = cmat_ref[...]
    nt_dims = (((1,), (1,)), ((), ()))
    first = [step * Q_CHAINS + c for c in range(Q_CHAINS)]

    def key_rows(j, c):
        packed = BLK // 2
        return pl.ds(pl.multiple_of(jnp.clip(j, 0, first[c]) * packed, packed), packed)

    def per_head_rows(ref, j, c, p):
        rows = key_rows(j, c)
        return jnp.concatenate(
            [pltpu.bitcast(ref[rows, (2 * p) * LANES:(2 * p + 1) * LANES], BF16),
             pltpu.bitcast(ref[rows, (2 * p + 1) * LANES:(2 * p + 2) * LANES], BF16)], axis=0)

    def scores(j, c, p):
        qp = q_ref[c * BLK:(c + 1) * BLK, p * LANES:(p + 1) * LANES]
        return lax.dot_general(qp, per_head_rows(k_ref, j, c, p), nt_dims,
                               preferred_element_type=F32)

    def apply_weights(j, c, p):
        u = c * HEAD_PAIRS + p
        acc_scr[u] += jnp.dot(w_scr[u], per_head_rows(v_ref, j, c, p), preferred_element_type=F32)

    for c in range(Q_CHAINS):
        for p in range(HEAD_PAIRS):
            z = scores(first[c], c, p)
            for hh in range(2):
                keys = slice(hh * BLK, (hh + 1) * BLK)
                z_scr[c * HEAD_PAIRS + p, :, keys] = jnp.where(before, z[:, keys], -MASKED)
    for u in range(Q_CHAINS * HEAD_PAIRS):
        acc_scr[u] = jnp.zeros((BLK, LANES), F32)

    def sweep(t, diagonal):
        m = None
        for c in range(Q_CHAINS):
            sps = []
            for h in range(ATTN_HEADS):
                z = z_scr[c * HEAD_PAIRS + h // 2, :, (h % 2) * BLK:(h % 2 + 1) * BLK]
                sp = jnp.maximum(z, 0.0) + jnp.log(1.0 + jnp.exp2(jnp.abs(z) * -LOG2_E))
                sps.append(sp.astype(BF16))
            cs = jnp.dot(jnp.concatenate(sps, axis=0), cmat,
                         preferred_element_type=F32)
            if not diagonal:
                for p in range(HEAD_PAIRS):
                    apply_weights(first[c] - t + 1, c, p)
            worst = None
            for h in range(ATTN_HEADS):
                n = c * ATTN_HEADS + h
                csn = cs[h * BLK:(h + 1) * BLK]
                z = z_scr[c * HEAD_PAIRS + h // 2, :, (h % 2) * BLK:(h % 2 + 1) * BLK]
                if diagonal:
                    r_new = csn[:, BLK:]
                    log_w = z + csn[:, :BLK]
                else:
                    r = r_scr[n]
                    r_new = r + csn[:, BLK:]
                    log_w = z + csn[:, :BLK] + r
                r_scr[n] = r_new
                worst = r_new if worst is None else jnp.maximum(worst, r_new)
                w_scr[c * HEAD_PAIRS + h // 2, :, (h % 2) * BLK:(h % 2 + 1) * BLK] = (
                    jnp.exp2(log_w * LOG2_E).astype(BF16))
            for p in range(HEAD_PAIRS):
                z_scr[c * HEAD_PAIRS + p] = scores(first[c] - t - 1, c, p)
            mc = jnp.where(first[c] - t - 1 >= 0, jnp.max(worst), -jnp.inf)
            m = mc if m is None else jnp.maximum(m, mc)
        @pl.when(first[0] - t - 1 < 0)
        def _():
            for c in range(Q_CHAINS):
                @pl.when(first[c] - t - 1 < 0)
                def _(c=c):
                    for h in range(ATTN_HEADS):
                        r_scr[c * ATTN_HEADS + h] = jnp.full((BLK, BLK), -MASKED, F32)
        return m

    def body(carry):
        t, _ = carry
        return t + 1, sweep(t, False)

    def cond(carry):
        return carry[1] > LOG_W_ZERO

    t_end, _ = lax.while_loop(cond, body, (jnp.int32(1), sweep(0, True)))
    for c in range(Q_CHAINS):
        for p in range(HEAD_PAIRS):
            apply_weights(first[c] - t_end + 1, c, p)

    for c in range(Q_CHAINS):
        o = jnp.concatenate([acc_scr[c * HEAD_PAIRS + p] for p in range(HEAD_PAIRS)], axis=1)
        o_ref[c * BLK:(c + 1) * BLK, :] = (_rms(o) * g_ref[...]).astype(BF16)


def _attention(q, k, v, g, cmat, batch, seq):
    n = q.shape[0]
    rows = Q_CHAINS * BLK
    steps = seq // rows
    qrow = lambda b, i: (b * steps + i, 0)
    kv = lambda b, i: (b, 0)
    fixed = lambda b, i: (0, 0)
    pairs = Q_CHAINS * HEAD_PAIRS
    return pl.pallas_call(
        _attn_kernel,
        grid=(batch, steps),
        in_specs=[pl.BlockSpec((rows, D_ATTN), qrow),
                  pl.BlockSpec((seq // 2, 2 * D_ATTN), kv),
                  pl.BlockSpec((seq // 2, 2 * D_ATTN), kv),
                  pl.BlockSpec((1, D_ATTN), fixed),
                  pl.BlockSpec((BLK, 2 * BLK), fixed)],
        out_specs=pl.BlockSpec((rows, D_ATTN), qrow),
        out_shape=jax.ShapeDtypeStruct((n, D_ATTN), BF16),
        scratch_shapes=[pltpu.VMEM((pairs, BLK, 2 * BLK), F32),
                        pltpu.VMEM((pairs, BLK, 2 * BLK), BF16),
                        pltpu.VMEM((Q_CHAINS * ATTN_HEADS, BLK, BLK), F32),
                        pltpu.VMEM((pairs, BLK, LANES), F32)],
        compiler_params=pltpu.CompilerParams(dimension_semantics=("arbitrary", "arbitrary"),
                                             vmem_limit_bytes=VMEM_LIMIT_BYTES),
        name="attention",
    )(q, k, v, g, cmat)


def _ssm_kernel(u_ref, bre_ref, bim_ref, lbre_ref, lbim_ref, cre_ref, cim_ref, d_ref,
                y_ref, utb_scr, ytb_scr, sr_scr, si_scr, *x_scr):
    xr_scr, xi_scr = x_scr[:SCAN_CHUNKS], x_scr[SCAN_CHUNKS:]

    @pl.when(pl.program_id(0) == 0)
    def _():
        sr_scr[...] = jnp.zeros_like(sr_scr)
        si_scr[...] = jnp.zeros_like(si_scr)

    for b in range(SUBLANES):
        for s in range(SSM_SLABS):
            utb_scr[s, pl.ds(b, SCAN_T, stride=SUBLANES), :] = u_ref[b, :, s * LANES:(s + 1) * LANES]

    slabs_per_slice = MXU_DIM // LANES
    chunks_per_slice = SCAN_CHUNKS // K_SLICES
    uk = [jnp.concatenate([utb_scr[k * slabs_per_slice + s] for s in range(slabs_per_slice)],
                          axis=1).astype(BF16) for k in range(K_SLICES)]

    def project_in(c):
        k, part = divmod(c, chunks_per_slice)
        cols = slice(part * SCAN_COLS, (part + 1) * SCAN_COLS)
        xr_scr[c][...] = jnp.dot(uk[k], bre_ref[k, :, cols], preferred_element_type=F32)
        xi_scr[c][...] = jnp.dot(uk[k], bim_ref[k, :, cols], preferred_element_type=F32)

    def recur(c):
        cols = slice(c * SCAN_COLS, (c + 1) * SCAN_COLS)
        lr = jnp.broadcast_to(lbre_ref[:, cols], (SUBLANES, SCAN_COLS))
        li = jnp.broadcast_to(lbim_ref[:, cols], (SUBLANES, SCAN_COLS))
        sr, si = sr_scr[:, cols], si_scr[:, cols]
        for t in range(SCAN_T):
            rows = slice(t * SUBLANES, (t + 1) * SUBLANES)
            sr, si = (lr * sr - li * si + xr_scr[c][rows, :],
                      lr * si + li * sr + xi_scr[c][rows, :])
            xr_scr[c][rows, :] = sr
            xi_scr[c][rows, :] = si
        sr_scr[:, cols] = sr
        si_scr[:, cols] = si

    def project_out(c):
        k, part = divmod(c, chunks_per_slice)
        rows = slice(part * SCAN_COLS, (part + 1) * SCAN_COLS)
        y = jnp.dot(xr_scr[c][...].astype(BF16), cre_ref[k, rows, :], preferred_element_type=F32)
        y = y - jnp.dot(xi_scr[c][...].astype(BF16), cim_ref[k, rows, :], preferred_element_type=F32)
        for s in range(slabs_per_slice):
            slab = k * slabs_per_slice + s
            lanes = slice(slab * LANES, (slab + 1) * LANES)
            ys = y[:, s * LANES:(s + 1) * LANES]
            if part == 0:
                ytb_scr[slab] = ys + d_ref[:, lanes] * utb_scr[slab]
            else:
                ytb_scr[slab] += ys

    for stage in range(SCAN_CHUNKS + 2):
        if stage < SCAN_CHUNKS:
            project_in(stage)
        if 1 <= stage <= SCAN_CHUNKS:
            recur(stage - 1)
        if stage >= 2:
            project_out(stage - 2)

    for b in range(SUBLANES):
        for s in range(SSM_SLABS):
            y_ref[b, :, s * LANES:(s + 1) * LANES] = ytb_scr[s, pl.ds(b, SCAN_T, stride=SUBLANES), :]


def _ssm_scan(u3, bre, bim, lbre, lbim, cre, cim, d):
    batch, seq, _ = u3.shape
    rows = SCAN_T * batch
    chunk = lambda i: (0, i, 0)
    fixed2 = lambda i: (0, 0)
    fixed3 = lambda i: (0, 0, 0)
    return pl.pallas_call(
        _ssm_kernel,
        grid=(seq // SCAN_T,),
        in_specs=[pl.BlockSpec((batch, SCAN_T, D_SSM), chunk),
                  pl.BlockSpec((K_SLICES, MXU_DIM, STATE_PER_SLICE), fixed3),
                  pl.BlockSpec((K_SLICES, MXU_DIM, STATE_PER_SLICE), fixed3),
                  pl.BlockSpec((1, N_STATE), fixed2),
                  pl.BlockSpec((1, N_STATE), fixed2),
                  pl.BlockSpec((K_SLICES, STATE_PER_SLICE, MXU_DIM), fixed3),
                  pl.BlockSpec((K_SLICES, STATE_PER_SLICE, MXU_DIM), fixed3),
                  pl.BlockSpec((1, D_SSM), fixed2)],
        out_specs=pl.BlockSpec((batch, SCAN_T, D_SSM), chunk),
        out_shape=jax.ShapeDtypeStruct((batch, seq, D_SSM), F32),
        scratch_shapes=[pltpu.VMEM((SSM_SLABS, rows, LANES), F32), pltpu.VMEM((SSM_SLABS, rows, LANES), F32),
                        pltpu.VMEM((SUBLANES, N_STATE), F32), pltpu.VMEM((SUBLANES, N_STATE), F32)]
        + [pltpu.VMEM((rows, SCAN_COLS), F32)] * (2 * SCAN_CHUNKS),
        compiler_params=pltpu.CompilerParams(dimension_semantics=("arbitrary",),
                                             vmem_limit_bytes=VMEM_LIMIT_BYTES),
        name="ssm_scan",
    )(u3, bre, bim, lbre, lbim, cre, cim, d)


def _post_kernel(x_ref, oa_ref, y_ref, wglu_ref, gs_ref, wout_ref, g2_ref,
                 wg_ref, wu_ref, wd_ref, gf_ref, o_ref):
    for s in range(TM_POST // SUB_ROWS):
        rows = slice(s * SUB_ROWS, (s + 1) * SUB_ROWS)
        yg = jax.nn.gelu(y_ref[rows, :])
        gate = jnp.dot(yg.astype(BF16), wglu_ref[...], preferred_element_type=F32)
        y2 = yg * jax.nn.sigmoid(gate)
        o_ssm = (_rms(y2) * gs_ref[...]).astype(BF16)
        x1 = x_ref[rows, :] + jnp.dot(oa_ref[rows, :], wout_ref[0:D_ATTN, :], preferred_element_type=F32)
        x1 = x1 + jnp.dot(o_ssm, wout_ref[D_ATTN:, :], preferred_element_type=F32)
        h2 = (_rms(x1) * g2_ref[...]).astype(BF16)
        o_ref[rows, :] = x1
        for c in range(D_FF // FF_CHUNK):
            cols = slice(c * FF_CHUNK, (c + 1) * FF_CHUNK)
            gt = jnp.dot(h2, wg_ref[:, cols], preferred_element_type=F32)
            up = jnp.dot(h2, wu_ref[:, cols], preferred_element_type=F32)
            a = (gt * jax.nn.sigmoid(gt) * up).astype(BF16)
            o_ref[rows, :] += jnp.dot(a, wd_ref[cols, :], preferred_element_type=F32)
        o_ref[rows, :] = _rms(o_ref[rows, :]) * gf_ref[...]


def _post(x2, oa, y, wglu, gs, wout, g2, wg, wu, wd, gf):
    n = x2.shape[0]
    row = lambda i: (i, 0)
    fixed = lambda i: (0, 0)
    once = functools.partial(pl.BlockSpec, index_map=fixed, pipeline_mode=pl.Buffered(1))
    return pl.pallas_call(
        _post_kernel,
        grid=(n // TM_POST,),
        in_specs=[pl.BlockSpec((TM_POST, D_MODEL), row),
                  pl.BlockSpec((TM_POST, D_ATTN), row),
                  pl.BlockSpec((TM_POST, D_SSM), row),
                  once((D_SSM, D_SSM)),
                  once((1, D_SSM)),
                  once((D_MODEL, D_MODEL)),
                  once((1, D_MODEL)),
                  once((D_MODEL, D_FF)),
                  once((D_MODEL, D_FF)),
                  once((D_FF, D_MODEL)),
                  once((1, D_MODEL))],
        out_specs=pl.BlockSpec((TM_POST, D_MODEL), row),
        out_shape=jax.ShapeDtypeStruct((n, D_MODEL), F32),
        compiler_params=pltpu.CompilerParams(dimension_semantics=("arbitrary",),
                                             vmem_limit_bytes=VMEM_LIMIT_BYTES),
        name="post",
    )(x2, oa, y, wglu, gs, wout, g2, wg, wu, wd, gf)


def _block_diag_b(bb):
    gl = MXU_DIM // SSM_GROUP
    eye = jnp.eye(gl, dtype=bb.dtype)
    t = bb.reshape(K_SLICES, gl, SSM_GROUP, SSM_STATE)
    m = t[:, :, :, None, :] * eye[None, :, None, :, None]
    return m.reshape(K_SLICES, MXU_DIM, STATE_PER_SLICE).astype(BF16)


def _block_diag_c(c):
    gl = MXU_DIM // SSM_GROUP
    eye = jnp.eye(gl, dtype=c.dtype)
    t = c.reshape(K_SLICES, gl, SSM_GROUP, SSM_STATE).transpose(0, 1, 3, 2)
    m = t[:, :, :, None, :] * eye[None, :, None, :, None]
    return m.reshape(K_SLICES, STATE_PER_SLICE, MXU_DIM).astype(BF16)


def _cumsum_matrix():
    j = jnp.arange(BLK)[:, None]
    s = jnp.arange(BLK)[None, :]
    tri = -(j >= s).astype(BF16)
    return jnp.concatenate([tri, -jnp.ones((BLK, BLK), BF16)], axis=1)


def kernel(x, norm1_g, w_in, attn_norm_g, lambda_re, lambda_im, log_step, b_re, b_im, c_re, c_im,
           d_skip, w_glu, ssm_norm_g, w_out, norm2_g, w_gate, w_up, w_down, final_norm_g):
    batch, seq, d_model = x.shape
    assert d_model == D_MODEL and norm1_g.shape[0] == 1 and batch == SUBLANES
    n = batch * seq
    x2 = x.reshape(n, D_MODEL)

    q, k, v, u = _in_proj(x2, norm1_g[0][None], w_in[0])
    oa = _attention(q, k, v, attn_norm_g[0][None], _cumsum_matrix(), batch, seq)

    lb_re, lb_im, bb_re, bb_im = _ssm_prep(
        lambda_re[0], lambda_im[0], log_step[0][:, None],
        b_re[0].transpose(0, 2, 1), b_im[0].transpose(0, 2, 1))
    y = _ssm_scan(u.reshape(batch, seq, D_SSM), _block_diag_b(bb_re), _block_diag_b(bb_im),
                  lb_re.reshape(1, N_STATE), lb_im.reshape(1, N_STATE),
                  _block_diag_c(c_re[0]), _block_diag_c(c_im[0]), d_skip[0].reshape(1, D_SSM))

    out = _post(x2, oa, y.reshape(n, D_SSM), w_glu[0].astype(BF16), ssm_norm_g[0][None],
                w_out[0].astype(BF16), norm2_g[0][None], w_gate[0].astype(BF16),
                w_up[0].astype(BF16), w_down[0].astype(BF16), final_norm_g[None])
    return out.reshape(batch, seq, D_MODEL)
```
